```python
import jax, jax.numpy as jnp
from jax import lax
import numpy as np

D_MODEL = 1024
BATCH = 4
SEQ = 8192
DEPTH = 1

RET_HEADS = 4
RET_DK = 128
RET_DV = 256
RET_CHUNK = 128
ROPE_BASE = 10000.0
MOBA_HEADS = 8
MOBA_DH = 64
MOBA_BLOCK = 256
MOBA_TOPK = 3
MOBA_QBLOCK = 64
FFN_HIDDEN = ((8 * D_MODEL + 3 * 256 - 1) // (3 * 256)) * 256

RET_QK = RET_HEADS * RET_DK
RET_V = RET_HEADS * RET_DV
MOBA_W = MOBA_HEADS * MOBA_DH
IN_SPLITS = (RET_QK, RET_QK, RET_V, RET_V, MOBA_W, MOBA_W, MOBA_W, D_MODEL, D_MODEL)
IN_COLS = sum(IN_SPLITS)

RMS_EPS = 1e-6
GN_EPS = 1e-5
NEG = -1e30

kernel_name = "hybrid_retention_moba_block"


def rmsnorm(x, w):
    xf = x.astype(jnp.float32)
    y = xf * lax.rsqrt(jnp.mean(xf * xf, axis=-1, keepdims=True) + RMS_EPS)
    return (y * w.astype(jnp.float32)).astype(x.dtype)


def rotary(x, pos):
    half = x.shape[-1] // 2
    inv = ROPE_BASE ** (-jnp.arange(half, dtype=jnp.float32) / half)
    ang = pos.astype(jnp.float32)[:, None] * inv[None, :]
    cos = jnp.cos(ang)[None, :, None, :].astype(x.dtype)
    sin = jnp.sin(ang)[None, :, None, :].astype(x.dtype)
    x1, x2 = x[..., :half], x[..., half:]
    return jnp.concatenate([x1 * cos - x2 * sin, x2 * cos + x1 * sin], axis=-1)


def retention(q, k, v):
    B, S, H, _ = q.shape
    dv = v.shape[-1]
    C = RET_CHUNK
    NC = S // C
    dt = q.dtype

    def chunks(t):
        return t.reshape(B, NC, C, H, t.shape[-1]).transpose(0, 3, 1, 2, 4)

    qc, kc, vc = chunks(q), chunks(k), chunks(v)
    lg = jnp.log1p(-jnp.exp2(-5.0 - jnp.arange(H, dtype=jnp.float32)))
    idx = jnp.arange(C, dtype=jnp.float32)
    diff = idx[:, None] - idx[None, :]
    dmat = jnp.where(diff >= 0, jnp.exp(jnp.maximum(diff, 0.0)[None] * lg[:, None, None]), 0.0).astype(dt)
    xi = jnp.exp((idx + 1.0)[None, :] * lg[:, None]).astype(dt)
    zeta = jnp.exp((C - 1.0 - idx)[None, :] * lg[:, None]).astype(dt)
    g_chunk = jnp.exp(C * lg).astype(dt)

    scores = jnp.einsum('bhcnd,bhcmd->bhcnm', qc, kc) * dmat[None, :, None]
    o_inner = jnp.einsum('bhcnm,bhcme->bhcne', scores, vc)
    upd = jnp.einsum('bhcmd,bhcme->bhcde', kc * zeta[None, :, None, :, None], vc)
    upd = jnp.moveaxis(upd, 2, 0)

    def step(state, u):
        return g_chunk[None, :, None, None] * state + u, state

    _, r_prev = lax.scan(step, jnp.zeros_like(upd[0]), upd)
    r_prev = jnp.moveaxis(r_prev, 0, 2)
    o_cross = jnp.einsum('bhcnd,bhcde->bhcne', qc * xi[None, :, None, :, None], r_prev)
    o = (o_inner + o_cross).transpose(0, 2, 3, 1, 4).reshape(B, S, H, dv)
    of = o.astype(jnp.float32)
    mu = jnp.mean(of, axis=-1, keepdims=True)
    var = jnp.mean(jnp.square(of - mu), axis=-1, keepdims=True)
    o = ((of - mu) * lax.rsqrt(var + GN_EPS)).astype(dt)
    return o.reshape(B, S, H * dv)


def gather_blocks(blocks, sel):
    return jax.vmap(jax.vmap(lambda bl, ix: bl[ix]))(blocks, sel)


def moba_attention(q, k, v):
    B, S, H, dh = q.shape
    BLK, QB = MOBA_BLOCK, MOBA_QBLOCK
    sp = ((S + BLK - 1) // BLK) * BLK
    pad = ((0, 0), (0, sp - S), (0, 0), (0, 0))
    qh = jnp.pad(q, pad).transpose(0, 2, 1, 3)
    kh = jnp.pad(k, pad).transpose(0, 2, 1, 3)
    vh = jnp.pad(v, pad).transpose(0, 2, 1, 3)
    nb = sp // BLK
    kb = kh.reshape(B, H, nb, BLK, dh)
    vb = vh.reshape(B, H, nb, BLK, dh)
    kbar = jnp.mean(kb, axis=3)
    topk = min(MOBA_TOPK, nb)
    scale = dh ** -0.5
    n_qb = sp // QB

    def one_block(qi):
        start = qi * QB
        blk = start // BLK
        qb = lax.dynamic_slice_in_dim(qh, start, QB, axis=2)
        gate = jnp.einsum('bhqd,bhnd->bhqn', qb, kbar).astype(jnp.float32)
        gate = jnp.where(jnp.arange(nb)[None, None, None, :] < blk, gate, NEG)
        _, sel = lax.top_k(gate, topk)
        valid = sel < blk
        ks = gather_blocks(kb, sel)
        vs = gather_blocks(vb, sel)
        s_sel = jnp.einsum('bhqd,bhqkjd->bhqkj', qb, ks).astype(jnp.float32) * scale
        s_sel = jnp.where(valid[..., None], s_sel, NEG).reshape(B, H, QB, topk * BLK)
        k_own = lax.dynamic_index_in_dim(kb, blk, axis=2, keepdims=False)
        v_own = lax.dynamic_index_in_dim(vb, blk, axis=2, keepdims=False)
        s_own = jnp.einsum('bhqd,bhjd->bhqj', qb, k_own).astype(jnp.float32) * scale
        qpos = start + jnp.arange(QB)
        kpos = blk * BLK + jnp.arange(BLK)
        s_own = jnp.where(kpos[None, :] <= qpos[:, None], s_own, NEG)
        p = jax.nn.softmax(jnp.concatenate([s_sel, s_own], axis=-1), axis=-1)
        p_sel = p[..., :topk * BLK].reshape(B, H, QB, topk, BLK).astype(v.dtype)
        p_own = p[..., topk * BLK:].astype(v.dtype)
        return (jnp.einsum('bhqkj,bhqkjd->bhqd', p_sel, vs)
                + jnp.einsum('bhqj,bhjd->bhqd', p_own, v_own))

    out = lax.map(one_block, jnp.arange(n_qb))
    out = out.transpose(1, 0, 3, 2, 4).reshape(B, sp, H * dh)
    return out[:, :S]


def setup_inputs(seed: int = 0) -> dict:
    key = jax.random.key(seed)
    ks = jax.random.split(key, 12)

    def w(k, shape, fan_in):
        return jax.random.normal(k, shape, jnp.float32) * fan_in ** -0.5

    def gain(k, shape):
        return 1.0 + 0.02 * jax.random.normal(k, shape, jnp.float32)

    return {
        "x": jax.random.normal(ks[0], (BATCH, SEQ, D_MODEL), jnp.float32),
        "norm1_w": gain(ks[1], (DEPTH, D_MODEL)),
        "w_in": w(ks[2], (DEPTH, D_MODEL, IN_COLS), D_MODEL),
        "q_norm_w": gain(ks[3], (DEPTH, MOBA_DH)),
        "k_norm_w": gain(ks[4], (DEPTH, MOBA_DH)),
        "w_ret_out": w(ks[5], (DEPTH, RET_V, D_MODEL), RET_V),
        "w_moba_out": w(ks[6], (DEPTH, MOBA_W, D_MODEL), MOBA_W),
        "w_o": w(ks[7], (DEPTH, D_MODEL, D_MODEL), D_MODEL),
        "norm2_w": gain(ks[8], (DEPTH, D_MODEL)),
        "w_ffn_gate": w(ks[9], (DEPTH, D_MODEL, FFN_HIDDEN), D_MODEL),
        "w_ffn_up": w(ks[10], (DEPTH, D_MODEL, FFN_HIDDEN), D_MODEL),
        "w_ffn_down": w(ks[11], (DEPTH, FFN_HIDDEN, D_MODEL), FFN_HIDDEN),
    }


def reference(x, norm1_w, w_in, q_norm_w, k_norm_w, w_ret_out, w_moba_out, w_o,
              norm2_w, w_ffn_gate, w_ffn_up, w_ffn_down):
    B, S, _ = x.shape
    pos = jnp.arange(S)
    split_points = [int(v) for v in np.cumsum(IN_SPLITS)[:-1]]
    for l in range(DEPTH):
        h = rmsnorm(x, norm1_w[l])
        proj = h @ w_in[l]
        rq, rk, rv, rg, mq, mk, mv, ga, gb = jnp.split(proj, split_points, axis=-1)

        rq = rotary(rq.reshape(B, S, RET_HEADS, RET_DK), pos)
        rk = rotary(rk.reshape(B, S, RET_HEADS, RET_DK), pos) * (RET_DK ** -0.5)
        rv = rv.reshape(B, S, RET_HEADS, RET_DV)
        ret = jax.nn.silu(rg) * retention(rq, rk, rv)
        a = ret @ w_ret_out[l]

        mq = rmsnorm(mq.reshape(B, S, MOBA_HEADS, MOBA_DH), q_norm_w[l])
        mk = rmsnorm(mk.reshape(B, S, MOBA_HEADS, MOBA_DH), k_norm_w[l])
        mv = mv.reshape(B, S, MOBA_HEADS, MOBA_DH)
        b = moba_attention(mq, mk, mv) @ w_moba_out[l]

        mix = jax.nn.sigmoid(ga) * a + jax.nn.sigmoid(gb) * b
        x = x + mix @ w_o[l]

        h2 = rmsnorm(x, norm2_w[l])
        f = (jax.nn.silu(h2 @ w_ffn_gate[l]) * (h2 @ w_ffn_up[l])) @ w_ffn_down[l]
        x = x + f
    return x
```

```python
import functools
import math

import numpy as np
import jax
import jax.numpy as jnp
from jax import lax
from jax.experimental import pallas as pl
from jax.experimental.pallas import tpu as pltpu

D_MODEL = 1024
RET_HEADS = 4
RET_DK = 128
RET_DV = 256
ROPE_BASE = 10000.0
MOBA_HEADS = 8
MOBA_DH = 64
MOBA_BLOCK = 256
MOBA_TOPK = 3
FFN_HIDDEN = 2816
RET_QK = RET_HEADS * RET_DK
RET_V = RET_HEADS * RET_DV
MOBA_W = MOBA_HEADS * MOBA_DH
IN_SPLITS = (RET_QK, RET_QK, RET_V, RET_V, MOBA_W, MOBA_W, MOBA_W, D_MODEL, D_MODEL)
IN_COLS = sum(IN_SPLITS)
RMS_EPS = 1e-6
GN_EPS = 1e-5
NEG = -1e30

LANES = 128
VMEM_LIMIT = 52 * 1024 * 1024
RET_CHUNK = 256

F32 = jnp.float32
BF16 = jnp.bfloat16


def _dot(a, b):
    return jnp.dot(a, b, preferred_element_type=F32)


def _dot_nt(a, b):
    return lax.dot_general(a, b, (((1,), (1,)), ((), ())), preferred_element_type=F32)


def _dot_tn(a, b):
    return lax.dot_general(a, b, (((0,), (0,)), ((), ())), preferred_element_type=F32)


def _const_spec(shape):
    nd = len(shape)
    return pl.BlockSpec(shape, lambda *_: (0,) * nd, pipeline_mode=pl.Buffered(1))


def _inproj_kernel(x_ref, n1_ref, w_ref, qn_ref, kn_ref, inv_ref, grp_ref,
                   rq_ref, rk_ref, rv_ref, rg_ref, mq_ref, mk_ref, mv_ref, ga_ref, gb_ref,
                   *, tm, seq):
    i = pl.program_id(0)
    x = x_ref[...]
    ms = jnp.mean(x * x, axis=-1, keepdims=True)
    h = (x * lax.rsqrt(ms + RMS_EPS) * n1_ref[...]).astype(BF16)

    offs = np.cumsum((0,) + IN_SPLITS)

    def proj(n):
        return _dot(h, w_ref[:, int(offs[n]):int(offs[n + 1])])

    pos0 = (i * tm) % seq
    pos = (pos0 + lax.broadcasted_iota(jnp.int32, (tm, LANES), 0)).astype(F32)
    lane = lax.broadcasted_iota(jnp.int32, (tm, LANES), 1)
    ang = pos * inv_ref[...]
    cos = jnp.cos(ang)
    sin = jnp.where(lane < RET_DK // 2, -jnp.sin(ang), jnp.sin(ang))

    def rotary_store(p, out_ref, scale):
        for hd in range(RET_HEADS):
            xh = p[:, hd * RET_DK:(hd + 1) * RET_DK]
            y = xh * cos + pltpu.roll(xh, RET_DK // 2, 1) * sin
            if scale is not None:
                y = y * scale
            out_ref[:, hd * RET_DK:(hd + 1) * RET_DK] = y.astype(BF16)

    rotary_store(proj(0), rq_ref, None)
    rotary_store(proj(1), rk_ref, RET_DK ** -0.5)
    rv_ref[...] = proj(2).astype(BF16)
    rg = proj(3)
    rg_ref[...] = (rg * jax.nn.sigmoid(rg)).astype(BF16)

    def head_rms_store(p, w_row_ref, out_ref):
        sq = p * p
        hi = sq.astype(BF16)
        lo = (sq - hi.astype(F32)).astype(BF16)
        msq = (_dot(hi, grp_ref[...]) + _dot(lo, grp_ref[...])) * (1.0 / MOBA_DH)
        out_ref[...] = (p * lax.rsqrt(msq + RMS_EPS) * w_row_ref[...]).astype(BF16)

    head_rms_store(proj(4), qn_ref, mq_ref)
    head_rms_store(proj(5), kn_ref, mk_ref)
    mv_ref[...] = proj(6).astype(BF16)
    ga_ref[...] = jax.nn.sigmoid(proj(7)).astype(BF16)
    gb_ref[...] = jax.nn.sigmoid(proj(8)).astype(BF16)


def _inproj(x2, norm1_w, w_in, q_norm_w, k_norm_w, *, seq, tm):
    T = x2.shape[0]
    half = RET_DK // 2
    inv = ROPE_BASE ** (-jnp.arange(half, dtype=F32) / half)
    inv2 = jnp.concatenate([inv, inv])[None, :]
    grp = jnp.asarray(np.kron(np.eye(MOBA_HEADS), np.ones((MOBA_DH, MOBA_DH))), BF16)
    qn = jnp.tile(q_norm_w.astype(F32), MOBA_HEADS)[None, :]
    kn = jnp.tile(k_norm_w.astype(F32), MOBA_HEADS)[None, :]
    widths = IN_SPLITS
    row = lambda w: pl.BlockSpec((tm, w), lambda i: (i, 0))
    return pl.pallas_call(
        functools.partial(_inproj_kernel, tm=tm, seq=seq),
        grid=(T // tm,),
        in_specs=[row(D_MODEL), _const_spec((1, D_MODEL)), _const_spec((D_MODEL, IN_COLS)),
                  _const_spec((1, MOBA_W)), _const_spec((1, MOBA_W)), _const_spec((1, LANES)),
                  _const_spec((MOBA_W, MOBA_W))],
        out_specs=[row(w) for w in widths],
        out_shape=[jax.ShapeDtypeStruct((T, w), BF16) for w in widths],
        compiler_params=pltpu.CompilerParams(dimension_semantics=("arbitrary",), vmem_limit_bytes=VMEM_LIMIT),
        name="inproj",
    )(x2, norm1_w.astype(F32)[None, :], w_in.astype(BF16), qn, kn, inv2, grp)


def _retention_kernel(q_ref, k_ref, v_ref, g_ref, dmat_ref, xi_ref, zeta_ref, o_ref, state_ref,
                      *, rt, chunk, g_chunk):
    @pl.when(pl.program_id(1) == 0)
    def _():
        state_ref[...] = jnp.zeros_like(state_ref)

    for c in range(rt // chunk):
        rows = slice(c * chunk, (c + 1) * chunk)
        for hd in range(RET_HEADS):
            kcols = slice(hd * RET_DK, (hd + 1) * RET_DK)
            vcols = slice(hd * RET_DV, (hd + 1) * RET_DV)
            q = q_ref[rows, kcols]
            k = k_ref[rows, kcols]
            v = v_ref[rows, vcols]
            state = state_ref[hd]
            scores = _dot_nt(q, k) * dmat_ref[hd]
            q_dec = (q.astype(F32) * xi_ref[hd]).astype(BF16)
            o = _dot(scores.astype(BF16), v) + _dot(q_dec, state.astype(BF16))
            k_dec = (k.astype(F32) * zeta_ref[hd]).astype(BF16)
            state_ref[hd] = g_chunk[hd] * state + _dot_tn(k_dec, v)
            mu = jnp.mean(o, axis=-1, keepdims=True)
            d = o - mu
            var = jnp.mean(d * d, axis=-1, keepdims=True)
            y = d * lax.rsqrt(var + GN_EPS)
            o_ref[rows, vcols] = (g_ref[rows, vcols].astype(F32) * y).astype(BF16)


def _retention(rq, rk, rv, rgs, *, batch, seq, rt):
    C = RET_CHUNK
    lg = np.log1p(-np.exp2(-5.0 - np.arange(RET_HEADS, dtype=np.float64)))
    idx = np.arange(C, dtype=np.float64)
    diff = idx[:, None] - idx[None, :]
    dmat = np.where(diff >= 0, np.exp(np.maximum(diff, 0.0)[None] * lg[:, None, None]), 0.0)
    xi = np.broadcast_to(np.exp((idx + 1.0)[None, :] * lg[:, None])[:, :, None], (RET_HEADS, C, LANES))
    zeta = np.broadcast_to(np.exp((C - 1.0 - idx)[None, :] * lg[:, None])[:, :, None], (RET_HEADS, C, LANES))
    g_chunk = tuple(float(v) for v in np.exp(C * lg))
    nt = seq // rt
    row = lambda w: pl.BlockSpec((rt, w), lambda b, j: (b * nt + j, 0))
    return pl.pallas_call(
        functools.partial(_retention_kernel, rt=rt, chunk=C, g_chunk=g_chunk),
        grid=(batch, nt),
        in_specs=[row(RET_QK), row(RET_QK), row(RET_V), row(RET_V),
                  _const_spec((RET_HEADS, C, C)), _const_spec((RET_HEADS, C, LANES)),
                  _const_spec((RET_HEADS, C, LANES))],
        out_specs=row(RET_V),
        out_shape=jax.ShapeDtypeStruct((batch * seq, RET_V), BF16),
        scratch_shapes=[pltpu.VMEM((RET_HEADS, RET_DK, RET_DV), F32)],
        compiler_params=pltpu.CompilerParams(dimension_semantics=("arbitrary", "arbitrary"),
                                             vmem_limit_bytes=VMEM_LIMIT),
        name="retention",
    )(rq, rk, rv, rgs, jnp.asarray(dmat, F32), jnp.asarray(xi, F32), jnp.asarray(zeta, F32))


def _moba_kernel(q_ref, k_ref, v_ref, o_ref, kbar_ref):
    BLK = MOBA_BLOCK
    i = pl.program_id(2)
    t = pl.program_id(3)
    lane = lax.broadcasted_iota(jnp.int32, (1, LANES), 1)
    mine = (lane >= MOBA_DH) == (t == 1)

    @pl.when(i == 0)
    def _():
        kbar_ref[t] = jnp.zeros((LANES, LANES), F32)

    q_pair = q_ref[...]
    q = jnp.where(mine, q_pair, jnp.zeros_like(q_pair))
    q_scaled = q * (MOBA_DH ** -0.5)
    row0 = pl.multiple_of(i * BLK, BLK)
    k_own = k_ref[pl.ds(row0, BLK), :]
    v_own = v_ref[pl.ds(row0, BLK), :]

    kbar = kbar_ref[t]
    kbar_hi = kbar.astype(BF16)
    kbar_lo = (kbar - kbar_hi.astype(F32)).astype(BF16)
    gate = _dot_nt(q, kbar_hi) + _dot_nt(q, kbar_lo)
    col = lax.broadcasted_iota(jnp.int32, (BLK, LANES), 1)
    past = col < i
    gate = jnp.where(past, gate, NEG)
    sel = jnp.zeros((BLK, LANES), F32)
    for _ in range(MOBA_TOPK):
        best = jnp.max(gate, axis=-1, keepdims=True)
        first = jnp.min(jnp.where(gate == best, col, LANES), axis=-1, keepdims=True)
        pick = col == first
        sel = jnp.where(pick & past, 1.0, sel)
        gate = jnp.where(pick, -3e38, gate)

    r_idx = lax.broadcasted_iota(jnp.int32, (BLK, BLK), 0)
    c_idx = lax.broadcasted_iota(jnp.int32, (BLK, BLK), 1)
    s = jnp.where(c_idx <= r_idx, _dot_nt(q_scaled, k_own), NEG)
    m = jnp.max(s, axis=-1, keepdims=True)
    p = jnp.exp(s - m)
    l = jnp.sum(p, axis=-1, keepdims=True)
    acc = _dot(p.astype(BF16), v_own)

    def body(j, carry):
        m, l, acc = carry
        r0 = pl.multiple_of(j * BLK, BLK)
        kj = k_ref[pl.ds(r0, BLK), :]
        vj = v_ref[pl.ds(r0, BLK), :]
        chosen = jnp.max(jnp.where(col == j, sel, 0.0), axis=-1, keepdims=True) > 0.5
        s = jnp.where(chosen, _dot_nt(q_scaled, kj), NEG)
        m_new = jnp.maximum(m, jnp.max(s, axis=-1, keepdims=True))
        alpha = jnp.exp(m - m_new)
        p = jnp.exp(s - m_new)
        l = alpha * l + jnp.sum(p, axis=-1, keepdims=True)
        acc = alpha * acc + _dot(p.astype(BF16), vj)
        return m_new, l, acc

    m, l, acc = lax.fori_loop(0, i, body, (m, l, acc))
    out = (acc / l).astype(BF16)

    @pl.when(t == 0)
    def _():
        o_ref[...] = out

    @pl.when(t == 1)
    def _():
        o_ref[...] = jnp.where(mine, out, o_ref[...])

    kbar_ref[t, pl.ds(i, 1), :] = jnp.mean(k_own.astype(F32), axis=0, keepdims=True)


def _moba(mq, mk, mv, *, batch, seq):
    BLK = MOBA_BLOCK
    nb = seq // BLK
    npair = MOBA_W // LANES
    qspec = pl.BlockSpec((BLK, LANES), lambda b, hp, i, t: (b * nb + i, hp))
    kvspec = pl.BlockSpec((seq, LANES), lambda b, hp, i, t: (b, hp))
    return pl.pallas_call(
        _moba_kernel,
        grid=(batch, npair, nb, 2),
        in_specs=[qspec, kvspec, kvspec],
        out_specs=qspec,
        out_shape=jax.ShapeDtypeStruct((batch * seq, MOBA_W), BF16),
        scratch_shapes=[pltpu.VMEM((2, LANES, LANES), F32)],
        compiler_params=pltpu.CompilerParams(dimension_semantics=("arbitrary",) * 4,
                                             vmem_limit_bytes=VMEM_LIMIT),
        name="moba",
    )(mq, mk, mv)


def _merge_kernel(x_ref, ret_ref, moba_ref, ga_ref, gb_ref, wr_ref, wm_ref, wo_ref, o_ref):
    a = _dot(ret_ref[...], wr_ref[...])
    b = _dot(moba_ref[...], wm_ref[...])
    mix = ga_ref[...].astype(F32) * a + gb_ref[...].astype(F32) * b
    o_ref[...] = x_ref[...] + _dot(mix.astype(BF16), wo_ref[...])


def _merge(x2, ret, moba, sga, sgb, w_ret_out, w_moba_out, w_o, *, tm):
    T = x2.shape[0]
    row = lambda w: pl.BlockSpec((tm, w), lambda i: (i, 0))
    return pl.pallas_call(
        _merge_kernel,
        grid=(T // tm,),
        in_specs=[row(D_MODEL), row(RET_V), row(MOBA_W), row(D_MODEL), row(D_MODEL),
                  _const_spec((RET_V, D_MODEL)), _const_spec((MOBA_W, D_MODEL)), _const_spec((D_MODEL, D_MODEL))],
        out_specs=row(D_MODEL),
        out_shape=jax.ShapeDtypeStruct((T, D_MODEL), F32),
        compiler_params=pltpu.CompilerParams(dimension_semantics=("arbitrary",), vmem_limit_bytes=VMEM_LIMIT),
        name="merge",
    )(x2, ret, moba, sga, sgb, w_ret_out.astype(BF16), w_moba_out.astype(BF16), w_o.astype(BF16))


def _ffn_kernel(x_ref, n2_ref, wg_ref, wu_ref, wd_ref, o_ref):
    x = x_ref[...]
    ms = jnp.mean(x * x, axis=-1, keepdims=True)
    h = (x * lax.rsqrt(ms + RMS_EPS) * n2_ref[...]).astype(BF16)
    g = _dot(h, wg_ref[...])
    u = _dot(h, wu_ref[...])
    act = (g * jax.nn.sigmoid(g) * u).astype(BF16)
    o_ref[...] = x + _dot(act, wd_ref[...])


def _ffn(x2, norm2_w, w_gate, w_up, w_down, *, tm):
    T = x2.shape[0]
    row = pl.BlockSpec((tm, D_MODEL), lambda i: (i, 0))
    return pl.pallas_call(
        _ffn_kernel,
        grid=(T // tm,),
        in_specs=[row, _const_spec((1, D_MODEL)), _const_spec((D_MODEL, FFN_HIDDEN)),
                  _const_spec((D_MODEL, FFN_HIDDEN)), _const_spec((FFN_HIDDEN, D_MODEL))],
        out_specs=row,
        out_shape=jax.ShapeDtypeStruct((T, D_MODEL), F32),
        compiler_params=pltpu.CompilerParams(dimension_semantics=("arbitrary",), vmem_limit_bytes=VMEM_LIMIT),
        name="ffn",
    )(x2, norm2_w.astype(F32)[None, :], w_gate.astype(BF16), w_up.astype(BF16), w_down.astype(BF16))


def kernel(x, norm1_w, w_in, q_norm_w, k_norm_w, w_ret_out, w_moba_out, w_o, norm2_w, w_ffn_gate, w_ffn_up, w_ffn_down):
    B, S, D = x.shape
    assert D == D_MODEL and S % MOBA_BLOCK == 0 and S % 512 == 0
    depth = norm1_w.shape[0]
    x2 = x.reshape(B * S, D)
    for l in range(depth):
        rq, rk, rv, rgs, mq, mk, mv, sga, sgb = _inproj(
            x2, norm1_w[l], w_in[l], q_norm_w[l], k_norm_w[l], seq=S, tm=512)
        ret = _retention(rq, rk, rv, rgs, batch=B, seq=S, rt=512)
        moba = _moba(mq, mk, mv, batch=B, seq=S)
        x2 = _merge(x2, ret, moba, sga, sgb, w_ret_out[l], w_moba_out[l], w_o[l], tm=512)
        x2 = _ffn(x2, norm2_w[l], w_ffn_gate[l], w_ffn_up[l], w_ffn_down[l], tm=512)
    return x2.reshape(B, S, D)
```

```python
import functools
import math

import numpy as np
import jax
import jax.numpy as jnp
from jax import lax
from jax.experimental import pallas as pl
from jax.experimental.pallas import tpu as pltpu

D_MODEL = 1024
RET_HEADS = 4
RET_DK = 128
RET_DV = 256
ROPE_BASE = 10000.0
MOBA_HEADS = 8
MOBA_DH = 64
MOBA_BLOCK = 256
MOBA_TOPK = 3
FFN_HIDDEN = 2816
RET_QK = RET_HEADS * RET_DK
RET_V = RET_HEADS * RET_DV
MOBA_W = MOBA_HEADS * MOBA_DH
IN_SPLITS = (RET_QK, RET_QK, RET_V, RET_V, MOBA_W, MOBA_W, MOBA_W, D_MODEL, D_MODEL)
IN_COLS = sum(IN_SPLITS)
RMS_EPS = 1e-6
GN_EPS = 1e-5
NEG = -1e30

LANES = 128
VMEM_LIMIT = 52 * 1024 * 1024
RET_CHUNK = 256

F32 = jnp.float32
BF16 = jnp.bfloat16


def _dot(a, b):
    return jnp.dot(a, b, preferred_element_type=F32)


def _dot_nt(a, b):
    return lax.dot_general(a, b, (((1,), (1,)), ((), ())), preferred_element_type=F32)


def _dot_tn(a, b):
    return lax.dot_general(a, b, (((0,), (0,)), ((), ())), preferred_element_type=F32)


def _const_spec(shape):
    nd = len(shape)
    return pl.BlockSpec(shape, lambda *_: (0,) * nd, pipeline_mode=pl.Buffered(1))


def _inproj_kernel(x_ref, n1_ref, w_ref, wvt_ref, qn_ref, kn_ref, inv_ref, grp_ref,
                   rq_ref, rk_ref, rv_ref, rg_ref, mq_ref, mk_ref, mvt_ref, ga_ref, gb_ref,
                   *, tm, seq):
    i = pl.program_id(0)
    x = x_ref[...]
    ms = jnp.mean(x * x, axis=-1, keepdims=True)
    h = (x * lax.rsqrt(ms + RMS_EPS) * n1_ref[...]).astype(BF16)

    offs = np.cumsum((0,) + IN_SPLITS)

    def proj(n):
        return _dot(h, w_ref[:, int(offs[n]):int(offs[n + 1])])

    pos0 = (i * tm) % seq
    pos = (pos0 + lax.broadcasted_iota(jnp.int32, (tm, LANES), 0)).astype(F32)
    lane = lax.broadcasted_iota(jnp.int32, (tm, LANES), 1)
    ang = pos * inv_ref[...]
    cos = jnp.cos(ang)
    sin = jnp.where(lane < RET_DK // 2, -jnp.sin(ang), jnp.sin(ang))

    def rotary_store(p, out_ref, scale):
        for hd in range(RET_HEADS):
            xh = p[:, hd * RET_DK:(hd + 1) * RET_DK]
            y = xh * cos + pltpu.roll(xh, RET_DK // 2, 1) * sin
            if scale is not None:
                y = y * scale
            out_ref[:, hd * RET_DK:(hd + 1) * RET_DK] = y.astype(BF16)

    rotary_store(proj(0), rq_ref, None)
    rotary_store(proj(1), rk_ref, RET_DK ** -0.5)
    rv_ref[...] = proj(2).astype(BF16)
    rg = proj(3)
    rg_ref[...] = (rg * jax.nn.sigmoid(rg)).astype(BF16)

    def head_rms_store(p, w_row_ref, out_ref):
        sq = p * p
        hi = sq.astype(BF16)
        lo = (sq - hi.astype(F32)).astype(BF16)
        msq = (_dot(hi, grp_ref[...]) + _dot(lo, grp_ref[...])) * (1.0 / MOBA_DH)
        out_ref[...] = (p * lax.rsqrt(msq + RMS_EPS) * w_row_ref[...]).astype(BF16)

    head_rms_store(proj(4), qn_ref, mq_ref)
    head_rms_store(proj(5), kn_ref, mk_ref)
    mvt = _dot_nt(wvt_ref[...], h).astype(BF16)
    for pair in range(MOBA_W // LANES):
        for c in range(tm // MOBA_BLOCK):
            mvt_ref[0, pair, c] = mvt[pair * LANES:(pair + 1) * LANES, c * MOBA_BLOCK:(c + 1) * MOBA_BLOCK]
    ga_ref[...] = jax.nn.sigmoid(proj(7)).astype(BF16)
    gb_ref[...] = jax.nn.sigmoid(proj(8)).astype(BF16)


def _inproj(x2, norm1_w, w_in, q_norm_w, k_norm_w, *, seq, tm):
    T = x2.shape[0]
    half = RET_DK // 2
    inv = ROPE_BASE ** (-jnp.arange(half, dtype=F32) / half)
    inv2 = jnp.concatenate([inv, inv])[None, :]
    grp = jnp.asarray(np.kron(np.eye(MOBA_HEADS), np.ones((MOBA_DH, MOBA_DH))), BF16)
    qn = jnp.tile(q_norm_w.astype(F32), MOBA_HEADS)[None, :]
    kn = jnp.tile(k_norm_w.astype(F32), MOBA_HEADS)[None, :]
    offs = np.cumsum((0,) + IN_SPLITS)
    w_mv_t = w_in[:, int(offs[6]):int(offs[7])].T.astype(BF16)
    nt = seq // tm
    npair = MOBA_W // LANES
    nb = seq // MOBA_BLOCK
    row = lambda w: pl.BlockSpec((tm, w), lambda i: (i, 0))
    row_out = lambda w: (row(w), jax.ShapeDtypeStruct((T, w), BF16))
    mvt_out = (pl.BlockSpec((1, npair, tm // MOBA_BLOCK, LANES, MOBA_BLOCK), lambda i: (i // nt, 0, i % nt, 0, 0)),
               jax.ShapeDtypeStruct((T // seq, npair, nb, LANES, MOBA_BLOCK), BF16))
    outs = [row_out(RET_QK), row_out(RET_QK), row_out(RET_V), row_out(RET_V), row_out(MOBA_W), row_out(MOBA_W),
            mvt_out, row_out(D_MODEL), row_out(D_MODEL)]
    return pl.pallas_call(
        functools.partial(_inproj_kernel, tm=tm, seq=seq),
        grid=(T // tm,),
        in_specs=[row(D_MODEL), _const_spec((1, D_MODEL)), _const_spec((D_MODEL, IN_COLS)),
                  _const_spec((MOBA_W, D_MODEL)),
                  _const_spec((1, MOBA_W)), _const_spec((1, MOBA_W)), _const_spec((1, LANES)),
                  _const_spec((MOBA_W, MOBA_W))],
        out_specs=[o[0] for o in outs],
        out_shape=[o[1] for o in outs],
        compiler_params=pltpu.CompilerParams(dimension_semantics=("arbitrary",), vmem_limit_bytes=VMEM_LIMIT),
        name="inproj",
    )(x2, norm1_w.astype(F32)[None, :], w_in.astype(BF16), w_mv_t, qn, kn, inv2, grp)


def _retention_kernel(q_ref, k_ref, v_ref, g_ref, dmat_ref, xi_ref, zeta_ref, o_ref, state_ref,
                      *, rt, chunk, g_chunk):
    @pl.when(pl.program_id(1) == 0)
    def _():
        state_ref[...] = jnp.zeros_like(state_ref)

    for c in range(rt // chunk):
        rows = slice(c * chunk, (c + 1) * chunk)
        for hd in range(RET_HEADS):
            kcols = slice(hd * RET_DK, (hd + 1) * RET_DK)
            vcols = slice(hd * RET_DV, (hd + 1) * RET_DV)
            q = q_ref[rows, kcols]
            k = k_ref[rows, kcols]
            v = v_ref[rows, vcols]
            state = state_ref[hd]
            scores = _dot_nt(q, k) * dmat_ref[hd]
            q_dec = (q.astype(F32) * xi_ref[hd]).astype(BF16)
            o = _dot(scores.astype(BF16), v) + _dot(q_dec, state.astype(BF16))
            k_dec = (k.astype(F32) * zeta_ref[hd]).astype(BF16)
            state_ref[hd] = g_chunk[hd] * state + _dot_tn(k_dec, v)
            mu = jnp.mean(o, axis=-1, keepdims=True)
            d = o - mu
            var = jnp.mean(d * d, axis=-1, keepdims=True)
            y = d * lax.rsqrt(var + GN_EPS)
            o_ref[rows, vcols] = (g_ref[rows, vcols].astype(F32) * y).astype(BF16)


def _retention(rq, rk, rv, rgs, *, batch, seq, rt):
    C = RET_CHUNK
    lg = np.log1p(-np.exp2(-5.0 - np.arange(RET_HEADS, dtype=np.float64)))
    idx = np.arange(C, dtype=np.float64)
    diff = idx[:, None] - idx[None, :]
    dmat = np.where(diff >= 0, np.exp(np.maximum(diff, 0.0)[None] * lg[:, None, None]), 0.0)
    xi = np.broadcast_to(np.exp((idx + 1.0)[None, :] * lg[:, None])[:, :, None], (RET_HEADS, C, LANES))
    zeta = np.broadcast_to(np.exp((C - 1.0 - idx)[None, :] * lg[:, None])[:, :, None], (RET_HEADS, C, LANES))
    g_chunk = tuple(float(v) for v in np.exp(C * lg))
    nt = seq // rt
    row = lambda w: pl.BlockSpec((rt, w), lambda b, j: (b * nt + j, 0))
    return pl.pallas_call(
        functools.partial(_retention_kernel, rt=rt, chunk=C, g_chunk=g_chunk),
        grid=(batch, nt),
        in_specs=[row(RET_QK), row(RET_QK), row(RET_V), row(RET_V),
                  _const_spec((RET_HEADS, C, C)), _const_spec((RET_HEADS, C, LANES)),
                  _const_spec((RET_HEADS, C, LANES))],
        out_specs=row(RET_V),
        out_shape=jax.ShapeDtypeStruct((batch * seq, RET_V), BF16),
        scratch_shapes=[pltpu.VMEM((RET_HEADS, RET_DK, RET_DV), F32)],
        compiler_params=pltpu.CompilerParams(dimension_semantics=("arbitrary", "arbitrary"),
                                             vmem_limit_bytes=VMEM_LIMIT),
        name="retention",
    )(rq, rk, rv, rgs, jnp.asarray(dmat, F32), jnp.asarray(xi, F32), jnp.asarray(zeta, F32))


def _moba_kernel(q_ref, k_ref, vt_ref, o_ref, kbar_ref, bias_ref, s_even_ref, s_odd_ref):
    BLK = MOBA_BLOCK
    nbp = kbar_ref.shape[0]
    nb = vt_ref.shape[2]
    i = pl.program_id(2)

    @pl.when(i == 0)
    def _():
        kbar_ref[...] = jnp.zeros_like(kbar_ref)

    lane = lax.broadcasted_iota(jnp.int32, (1, LANES), 1)
    q_pair = q_ref[...] * (MOBA_DH ** -0.5)
    zero = jnp.zeros_like(q_pair)
    q_cat = jnp.concatenate([jnp.where(lane < MOBA_DH, q_pair, zero),
                             jnp.where(lane >= MOBA_DH, q_pair, zero)], axis=0)

    def scores(j, dst_ref):
        kj = k_ref[pl.ds(pl.multiple_of(j * BLK, BLK), BLK), :]
        dst_ref[...] = _dot_nt(kj, q_cat)

    scores(0, s_even_ref)

    kbar = kbar_ref[...]
    kbar_hi = kbar.astype(BF16)
    kbar_lo = (kbar - kbar_hi.astype(F32)).astype(BF16)
    gate = _dot_nt(kbar_hi, q_cat) + _dot_nt(kbar_lo, q_cat)
    blk_idx = lax.broadcasted_iota(jnp.int32, (nbp, 2 * BLK), 0)
    past = blk_idx < i
    gate = jnp.where(past, gate, NEG)
    sel = jnp.zeros((nbp, 2 * BLK), jnp.bool_)
    for _ in range(MOBA_TOPK):
        best = jnp.max(gate, axis=0, keepdims=True)
        first = jnp.min(jnp.where(gate == best, blk_idx, nbp), axis=0, keepdims=True)
        pick = blk_idx == first
        sel = sel | (pick & past)
        gate = jnp.where(pick, -3e38, gate)
    bias_ref[...] = jnp.where(sel, 0.0, NEG)

    def attend(s, j, carry):
        m, l, acc0, acc1 = carry
        m_new = jnp.maximum(m, jnp.max(s, axis=0, keepdims=True))
        alpha = jnp.exp(m - m_new)
        p = jnp.exp(s - m_new)
        l = alpha * l + jnp.sum(p, axis=0, keepdims=True)
        pv = _dot(vt_ref[0, 0, j], p.astype(BF16))
        acc0 = alpha[:, :BLK] * acc0 + pv[:MOBA_DH, :BLK]
        acc1 = alpha[:, BLK:] * acc1 + pv[MOBA_DH:, BLK:]
        return m_new, l, acc0, acc1

    row0 = pl.multiple_of(i * BLK, BLK)
    k_own = k_ref[pl.ds(row0, BLK), :]
    key_idx = lax.broadcasted_iota(jnp.int32, (BLK, 2 * BLK), 0)
    qry_idx = lax.broadcasted_iota(jnp.int32, (BLK, 2 * BLK), 1) & (BLK - 1)
    s_own = jnp.where(key_idx <= qry_idx, _dot_nt(k_own, q_cat), NEG)
    init = (jnp.full((1, 2 * BLK), NEG, F32), jnp.zeros((1, 2 * BLK), F32),
            jnp.zeros((MOBA_DH, BLK), F32), jnp.zeros((MOBA_DH, BLK), F32))
    carry = attend(s_own, i, init)

    def body(jj, carry):
        j0 = 2 * jj
        scores(j0 + 1, s_odd_ref)
        carry = attend(s_even_ref[...] + bias_ref[pl.ds(j0, 1), :], j0, carry)
        scores(jnp.minimum(j0 + 2, nb - 1), s_even_ref)
        carry = attend(s_odd_ref[...] + bias_ref[pl.ds(j0 + 1, 1), :], j0 + 1, carry)
        return carry

    _, l, acc0, acc1 = lax.fori_loop(0, (i + 1) // 2, body, carry)
    out_t = jnp.concatenate([acc0 / l[:, :BLK], acc1 / l[:, BLK:]], axis=0)
    o_ref[...] = out_t.T.astype(BF16)
    kbar_ref[pl.ds(i, 1), :] = jnp.mean(k_own.astype(F32), axis=0, keepdims=True)


def _moba(mq, mk, mvt, *, batch, seq):
    BLK = MOBA_BLOCK
    nb = seq // BLK
    nbp = -(-(nb + 1) // 8) * 8
    npair = MOBA_W // LANES
    qspec = pl.BlockSpec((BLK, LANES), lambda b, hp, i: (b * nb + i, hp))
    kspec = pl.BlockSpec((seq, LANES), lambda b, hp, i: (b, hp))
    vtspec = pl.BlockSpec((1, 1, nb, LANES, BLK), lambda b, hp, i: (b, hp, 0, 0, 0))
    return pl.pallas_call(
        _moba_kernel,
        grid=(batch, npair, nb),
        in_specs=[qspec, kspec, vtspec],
        out_specs=qspec,
        out_shape=jax.ShapeDtypeStruct((batch * seq, MOBA_W), BF16),
        scratch_shapes=[pltpu.VMEM((nbp, LANES), F32), pltpu.VMEM((nbp, 2 * BLK), F32),
                        pltpu.VMEM((BLK, 2 * BLK), F32), pltpu.VMEM((BLK, 2 * BLK), F32)],
        compiler_params=pltpu.CompilerParams(dimension_semantics=("arbitrary",) * 3,
                                             vmem_limit_bytes=VMEM_LIMIT),
        name="moba",
    )(mq, mk, mvt)


def _merge_kernel(x_ref, ret_ref, moba_ref, ga_ref, gb_ref, wr_ref, wm_ref, wo_ref, o_ref):
    a = _dot(ret_ref[...], wr_ref[...])
    b = _dot(moba_ref[...], wm_ref[...])
    mix = ga_ref[...].astype(F32) * a + gb_ref[...].astype(F32) * b
    o_ref[...] = x_ref[...] + _dot(mix.astype(BF16), wo_ref[...])


def _merge(x2, ret, moba, sga, sgb, w_ret_out, w_moba_out, w_o, *, tm):
    T = x2.shape[0]
    row = lambda w: pl.BlockSpec((tm, w), lambda i: (i, 0))
    return pl.pallas_call(
        _merge_kernel,
        grid=(T // tm,),
        in_specs=[row(D_MODEL), row(RET_V), row(MOBA_W), row(D_MODEL), row(D_MODEL),
                  _const_spec((RET_V, D_MODEL)), _const_spec((MOBA_W, D_MODEL)), _const_spec((D_MODEL, D_MODEL))],
        out_specs=row(D_MODEL),
        out_shape=jax.ShapeDtypeStruct((T, D_MODEL), F32),
        compiler_params=pltpu.CompilerParams(dimension_semantics=("arbitrary",), vmem_limit_bytes=VMEM_LIMIT),
        name="merge",
    )(x2, ret, moba, sga, sgb, w_ret_out.astype(BF16), w_moba_out.astype(BF16), w_o.astype(BF16))


def _ffn_kernel(x_ref, n2_ref, wg_ref, wu_ref, wd_ref, o_ref):
    x = x_ref[...]
    ms = jnp.mean(x * x, axis=-1, keepdims=True)
    h = (x * lax.rsqrt(ms + RMS_EPS) * n2_ref[...]).astype(BF16)
    g = _dot(h, wg_ref[...])
    u = _dot(h, wu_ref[...])
    act = (g * jax.nn.sigmoid(g) * u).astype(BF16)
    o_ref[...] = x + _dot(act, wd_ref[...])


def _ffn(x2, norm2_w, w_gate, w_up, w_down, *, tm):
    T = x2.shape[0]
    row = pl.BlockSpec((tm, D_MODEL), lambda i: (i, 0))
    return pl.pallas_call(
        _ffn_kernel,
        grid=(T // tm,),
        in_specs=[row, _const_spec((1, D_MODEL)), _const_spec((D_MODEL, FFN_HIDDEN)),
                  _const_spec((D_MODEL, FFN_HIDDEN)), _const_spec((FFN_HIDDEN, D_MODEL))],
        out_specs=row,
        out_shape=jax.ShapeDtypeStruct((T, D_MODEL), F32),
        compiler_params=pltpu.CompilerParams(dimension_semantics=("arbitrary",), vmem_limit_bytes=VMEM_LIMIT),
        name="ffn",
    )(x2, norm2_w.astype(F32)[None, :], w_gate.astype(BF16), w_up.astype(BF16), w_down.astype(BF16))


def kernel(x, norm1_w, w_in, q_norm_w, k_norm_w, w_ret_out, w_moba_out, w_o, norm2_w, w_ffn_gate, w_ffn_up, w_ffn_down):
    B, S, D = x.shape
    assert D == D_MODEL and S % MOBA_BLOCK == 0 and S % 512 == 0
    depth = norm1_w.shape[0]
    x2 = x.reshape(B * S, D)
    for l in range(depth):
        rq, rk, rv, rgs, mq, mk, mv, sga, sgb = _inproj(
            x2, norm1_w[l], w_in[l], q_norm_w[l], k_norm_w[l], seq=S, tm=512)
        ret = _retention(rq, rk, rv, rgs, batch=B, seq=S, rt=512)
        moba = _moba(mq, mk, mv, batch=B, seq=S)
        x2 = _merge(x2, ret, moba, sga, sgb, w_ret_out[l], w_moba_out[l], w_o[l], tm=512)
        x2 = _ffn(x2, norm2_w[l], w_ffn_gate[l], w_ffn_up[l], w_ffn_down[l], tm=512)
    return x2.reshape(B, S, D)
```

```python
import functools
import math

import numpy as np
import jax
import jax.numpy as jnp
from jax import lax
from jax.experimental import pallas as pl
from jax.experimental.pallas import tpu as pltpu

D_MODEL = 1024
RET_HEADS = 4
RET_DK = 128
RET_DV = 256
ROPE_BASE = 10000.0
MOBA_HEADS = 8
MOBA_DH = 64
MOBA_BLOCK = 256
MOBA_TOPK = 3
FFN_HIDDEN = 2816
RET_QK = RET_HEADS * RET_DK
RET_V = RET_HEADS * RET_DV
MOBA_W = MOBA_HEADS * MOBA_DH
IN_SPLITS = (RET_QK, RET_QK, RET_V, RET_V, MOBA_W, MOBA_W, MOBA_W, D_MODEL, D_MODEL)
IN_COLS = sum(IN_SPLITS)
RMS_EPS = 1e-6
GN_EPS = 1e-5
NEG = -1e30

LANES = 128
VMEM_LIMIT = 52 * 1024 * 1024
RET_CHUNK = 256

F32 = jnp.float32
BF16 = jnp.bfloat16


def _dot(a, b):
    return jnp.dot(a, b, preferred_element_type=F32)


def _dot_nt(a, b):
    return lax.dot_general(a, b, (((1,), (1,)), ((), ())), preferred_element_type=F32)


def _dot_tn(a, b):
    return lax.dot_general(a, b, (((0,), (0,)), ((), ())), preferred_element_type=F32)


def _const_spec(shape):
    nd = len(shape)
    return pl.BlockSpec(shape, lambda *_: (0,) * nd, pipeline_mode=pl.Buffered(1))


def _inproj_kernel(x_ref, n1_ref, w_ref, wvt_ref, qn_ref, kn_ref, inv_ref, grp_ref,
                   rq_ref, rk_ref, rv_ref, rg_ref, mq_ref, mk_ref, mvt_ref, ga_ref, gb_ref,
                   *, tm, seq):
    i = pl.program_id(0)
    x = x_ref[...]
    ms = jnp.mean(x * x, axis=-1, keepdims=True)
    h = (x * lax.rsqrt(ms + RMS_EPS) * n1_ref[...]).astype(BF16)

    offs = np.cumsum((0,) + IN_SPLITS)

    def proj(n):
        return _dot(h, w_ref[:, int(offs[n]):int(offs[n + 1])])

    pos0 = (i * tm) % seq
    pos = (pos0 + lax.broadcasted_iota(jnp.int32, (tm, LANES), 0)).astype(F32)
    lane = lax.broadcasted_iota(jnp.int32, (tm, LANES), 1)
    ang = pos * inv_ref[...]
    cos = jnp.cos(ang)
    sin = jnp.where(lane < RET_DK // 2, -jnp.sin(ang), jnp.sin(ang))

    def rotary_store(p, out_ref, scale):
        for hd in range(RET_HEADS):
            xh = p[:, hd * RET_DK:(hd + 1) * RET_DK]
            y = xh * cos + pltpu.roll(xh, RET_DK // 2, 1) * sin
            if scale is not None:
                y = y * scale
            out_ref[:, hd * RET_DK:(hd + 1) * RET_DK] = y.astype(BF16)

    rotary_store(proj(0), rq_ref, None)
    rotary_store(proj(1), rk_ref, RET_DK ** -0.5)
    rv_ref[...] = proj(2).astype(BF16)
    rg = proj(3)
    rg_ref[...] = (rg * jax.nn.sigmoid(rg)).astype(BF16)

    def head_rms_store(p, w_row_ref, out_ref, scale):
        sq = p * p
        hi = sq.astype(BF16)
        lo = (sq - hi.astype(F32)).astype(BF16)
        msq = (_dot(hi, grp_ref[...]) + _dot(lo, grp_ref[...])) * (1.0 / MOBA_DH)
        y = p * lax.rsqrt(msq + RMS_EPS) * w_row_ref[...]
        out_ref[...] = (y if scale is None else y * scale).astype(BF16)

    head_rms_store(proj(4), qn_ref, mq_ref, MOBA_DH ** -0.5 * math.log2(math.e))
    head_rms_store(proj(5), kn_ref, mk_ref, None)
    mvt = _dot_nt(wvt_ref[...], h).astype(BF16)
    for pair in range(MOBA_W // LANES):
        for c in range(tm // MOBA_BLOCK):
            mvt_ref[0, pair, c] = mvt[pair * LANES:(pair + 1) * LANES, c * MOBA_BLOCK:(c + 1) * MOBA_BLOCK]
    ga_ref[...] = jax.nn.sigmoid(proj(7)).astype(BF16)
    gb_ref[...] = jax.nn.sigmoid(proj(8)).astype(BF16)


def _inproj(x2, norm1_w, w_in, q_norm_w, k_norm_w, *, seq, tm):
    T = x2.shape[0]
    half = RET_DK // 2
    inv = ROPE_BASE ** (-jnp.arange(half, dtype=F32) / half)
    inv2 = jnp.concatenate([inv, inv])[None, :]
    grp = jnp.asarray(np.kron(np.eye(MOBA_HEADS), np.ones((MOBA_DH, MOBA_DH))), BF16)
    qn = jnp.tile(q_norm_w.astype(F32), MOBA_HEADS)[None, :]
    kn = jnp.tile(k_norm_w.astype(F32), MOBA_HEADS)[None, :]
    offs = np.cumsum((0,) + IN_SPLITS)
    w_mv_t = w_in[:, int(offs[6]):int(offs[7])].T.astype(BF16)
    nt = seq // tm
    npair = MOBA_W // LANES
    nb = seq // MOBA_BLOCK
    row = lambda w: pl.BlockSpec((tm, w), lambda i: (i, 0))
    row_out = lambda w: (row(w), jax.ShapeDtypeStruct((T, w), BF16))
    mvt_out = (pl.BlockSpec((1, npair, tm // MOBA_BLOCK, LANES, MOBA_BLOCK), lambda i: (i // nt, 0, i % nt, 0, 0)),
               jax.ShapeDtypeStruct((T // seq, npair, nb, LANES, MOBA_BLOCK), BF16))
    outs = [row_out(RET_QK), row_out(RET_QK), row_out(RET_V), row_out(RET_V), row_out(MOBA_W), row_out(MOBA_W),
            mvt_out, row_out(D_MODEL), row_out(D_MODEL)]
    return pl.pallas_call(
        functools.partial(_inproj_kernel, tm=tm, seq=seq),
        grid=(T // tm,),
        in_specs=[row(D_MODEL), _const_spec((1, D_MODEL)), _const_spec((D_MODEL, IN_COLS)),
                  _const_spec((MOBA_W, D_MODEL)),
                  _const_spec((1, MOBA_W)), _const_spec((1, MOBA_W)), _const_spec((1, LANES)),
                  _const_spec((MOBA_W, MOBA_W))],
        out_specs=[o[0] for o in outs],
        out_shape=[o[1] for o in outs],
        compiler_params=pltpu.CompilerParams(dimension_semantics=("arbitrary",), vmem_limit_bytes=VMEM_LIMIT),
        name="inproj",
    )(x2, norm1_w.astype(F32)[None, :], w_in.astype(BF16), w_mv_t, qn, kn, inv2, grp)


def _retention_kernel(q_ref, k_ref, v_ref, g_ref, dmat_ref, xi_ref, zeta_ref, o_ref, state_ref,
                      *, rt, chunk, g_chunk):
    @pl.when(pl.program_id(1) == 0)
    def _():
        state_ref[...] = jnp.zeros_like(state_ref)

    for c in range(rt // chunk):
        rows = slice(c * chunk, (c + 1) * chunk)
        for hd in range(RET_HEADS):
            kcols = slice(hd * RET_DK, (hd + 1) * RET_DK)
            vcols = slice(hd * RET_DV, (hd + 1) * RET_DV)
            q = q_ref[rows, kcols]
            k = k_ref[rows, kcols]
            v = v_ref[rows, vcols]
            state = state_ref[hd]
            scores = _dot_nt(q, k) * dmat_ref[hd]
            q_dec = (q.astype(F32) * xi_ref[hd]).astype(BF16)
            o = _dot(scores.astype(BF16), v) + _dot(q_dec, state.astype(BF16))
            k_dec = (k.astype(F32) * zeta_ref[hd]).astype(BF16)
            state_ref[hd] = g_chunk[hd] * state + _dot_tn(k_dec, v)
            mu = jnp.mean(o, axis=-1, keepdims=True)
            d = o - mu
            var = jnp.mean(d * d, axis=-1, keepdims=True)
            y = d * lax.rsqrt(var + GN_EPS)
            o_ref[rows, vcols] = (g_ref[rows, vcols].astype(F32) * y).astype(BF16)


def _retention(rq, rk, rv, rgs, *, batch, seq, rt):
    C = RET_CHUNK
    lg = np.log1p(-np.exp2(-5.0 - np.arange(RET_HEADS, dtype=np.float64)))
    idx = np.arange(C, dtype=np.float64)
    diff = idx[:, None] - idx[None, :]
    dmat = np.where(diff >= 0, np.exp(np.maximum(diff, 0.0)[None] * lg[:, None, None]), 0.0)
    xi = np.broadcast_to(np.exp((idx + 1.0)[None, :] * lg[:, None])[:, :, None], (RET_HEADS, C, LANES))
    zeta = np.broadcast_to(np.exp((C - 1.0 - idx)[None, :] * lg[:, None])[:, :, None], (RET_HEADS, C, LANES))
    g_chunk = tuple(float(v) for v in np.exp(C * lg))
    nt = seq // rt
    row = lambda w: pl.BlockSpec((rt, w), lambda b, j: (b * nt + j, 0))
    return pl.pallas_call(
        functools.partial(_retention_kernel, rt=rt, chunk=C, g_chunk=g_chunk),
        grid=(batch, nt),
        in_specs=[row(RET_QK), row(RET_QK), row(RET_V), row(RET_V),
                  _const_spec((RET_HEADS, C, C)), _const_spec((RET_HEADS, C, LANES)),
                  _const_spec((RET_HEADS, C, LANES))],
        out_specs=row(RET_V),
        out_shape=jax.ShapeDtypeStruct((batch * seq, RET_V), BF16),
        scratch_shapes=[pltpu.VMEM((RET_HEADS, RET_DK, RET_DV), F32)],
        compiler_params=pltpu.CompilerParams(dimension_semantics=("arbitrary", "arbitrary"),
                                             vmem_limit_bytes=VMEM_LIMIT),
        name="retention",
    )(rq, rk, rv, rgs, jnp.asarray(dmat, F32), jnp.asarray(xi, F32), jnp.asarray(zeta, F32))


def _moba_kernel(q_ref, k_ref, vt_ref, o_ref, kbar_ref, bias_ref,
                 s_even_ref, s_odd_ref, p_even_ref, p_odd_ref):
    BLK = MOBA_BLOCK
    DH = MOBA_DH
    STG = 2 * BLK
    nbp = kbar_ref.shape[0]
    n_stage = vt_ref.shape[2] // 2
    i = pl.program_id(2)

    @pl.when(i == 0)
    def _():
        kbar_ref[...] = jnp.zeros_like(kbar_ref)

    lane = lax.broadcasted_iota(jnp.int32, (1, LANES), 1)
    q_pair = q_ref[...]
    zero = jnp.zeros_like(q_pair)
    q_cat = jnp.concatenate([jnp.where(lane < DH, q_pair, zero),
                             jnp.where(lane >= DH, q_pair, zero)], axis=0)

    def scores(st, dst_ref):
        ks = k_ref[pl.ds(pl.multiple_of(st * STG, STG), STG), :]
        dst_ref[...] = _dot_nt(ks, q_cat)

    def probs(s_ref, st, m, p_ref):
        b0 = bias_ref[pl.ds(2 * st, 1), :]
        b1 = bias_ref[pl.ds(2 * st + 1, 1), :]
        s0 = s_ref[:BLK, :]
        s1 = s_ref[BLK:, :]
        m_new = jnp.maximum(m, jnp.maximum(jnp.max(s0, axis=0, keepdims=True) + b0,
                                           jnp.max(s1, axis=0, keepdims=True) + b1))
        p_ref[:BLK, :] = jnp.exp2(s0 - (m_new - b0)).astype(BF16)
        p_ref[BLK:, :] = jnp.exp2(s1 - (m_new - b1)).astype(BF16)
        return m_new

    def values(vt, p):
        return _dot(jnp.concatenate([vt, jnp.ones((16, vt.shape[1]), BF16)], axis=0), p)

    def stage_values(st, p_ref):
        return values(jnp.concatenate([vt_ref[0, 0, 2 * st], vt_ref[0, 0, 2 * st + 1]], axis=1), p_ref[...])

    def fold(pv, m_pv, state):
        r, l, acc0, acc1 = state
        scale = jnp.exp2(r - m_pv)
        return (m_pv, scale * l + pv[2 * DH:2 * DH + 1, :],
                scale[:, :BLK] * acc0 + pv[:DH, :BLK], scale[:, BLK:] * acc1 + pv[DH:2 * DH, BLK:])

    row0 = pl.multiple_of(i * BLK, BLK)
    k_own = k_ref[pl.ds(row0, BLK), :]
    s_own = _dot_nt(k_own, q_cat)
    scores(0, s_even_ref)
    p_odd_ref[...] = jnp.zeros_like(p_odd_ref)

    kbar = kbar_ref[...]
    kbar_hi = kbar.astype(BF16)
    kbar_lo = (kbar - kbar_hi.astype(F32)).astype(BF16)
    gate = _dot_nt(kbar_hi, q_cat) + _dot_nt(kbar_lo, q_cat)

    key_idx = lax.broadcasted_iota(jnp.int32, (BLK, 2 * BLK), 0)
    qry_idx = lax.broadcasted_iota(jnp.int32, (BLK, 2 * BLK), 1) & (BLK - 1)
    s_own = jnp.where(key_idx <= qry_idx, s_own, NEG)
    m_own = jnp.max(s_own, axis=0, keepdims=True)
    pv_own = values(vt_ref[0, 0, i], jnp.exp2(s_own - m_own).astype(BF16))

    blk_idx = lax.broadcasted_iota(jnp.int32, (nbp, 2 * BLK), 0)
    past = blk_idx < i
    gate = jnp.where(past, gate, NEG)
    sel = jnp.zeros((nbp, 2 * BLK), jnp.bool_)
    for _ in range(MOBA_TOPK):
        best = jnp.max(gate, axis=0, keepdims=True)
        first = jnp.min(jnp.where(gate == best, blk_idx, nbp), axis=0, keepdims=True)
        pick = blk_idx == first
        sel = sel | (pick & past)
        gate = jnp.where(pick, -3e38, gate)
    bias_ref[...] = jnp.where(sel, 0.0, NEG)

    state = (m_own, pv_own[2 * DH:2 * DH + 1, :], pv_own[:DH, :BLK], pv_own[DH:2 * DH, BLK:])

    def body(it, carry):
        m, m_pend, state = carry
        st_e = 2 * it
        st_o = st_e + 1
        pv_pend = stage_values(jnp.maximum(st_o - 2, 0), p_odd_ref)
        scores(st_o, s_odd_ref)
        m_e = probs(s_even_ref, st_e, m, p_even_ref)
        state = fold(pv_pend, m_pend, state)
        pv_e = stage_values(st_e, p_even_ref)
        scores(jnp.minimum(st_e + 2, n_stage - 1), s_even_ref)
        m_o = probs(s_odd_ref, st_o, m_e, p_odd_ref)
        state = fold(pv_e, m_e, state)
        return m_o, m_o, state

    trips = (i + 3) // 4
    _, m_pend, state = lax.fori_loop(0, trips, body, (m_own, m_own, state))
    pv_pend = stage_values(jnp.maximum(2 * trips - 1, 0), p_odd_ref)
    _, l, acc0, acc1 = fold(pv_pend, m_pend, state)
    out_t = jnp.concatenate([acc0 / l[:, :BLK], acc1 / l[:, BLK:]], axis=0)
    o_ref[...] = out_t.T.astype(BF16)
    kbar_ref[pl.ds(i, 1), :] = jnp.mean(k_own.astype(F32), axis=0, keepdims=True)


def _moba(mq, mk, mvt, *, batch, seq):
    BLK = MOBA_BLOCK
    nb = seq // BLK
    assert nb % 4 == 0
    nbp = -(-nb // 8) * 8
    npair = MOBA_W // LANES
    qspec = pl.BlockSpec((BLK, LANES), lambda b, hp, i: (b * nb + i, hp))
    kspec = pl.BlockSpec((seq, LANES), lambda b, hp, i: (b, hp))
    vtspec = pl.BlockSpec((1, 1, nb, LANES, BLK), lambda b, hp, i: (b, hp, 0, 0, 0))
    return pl.pallas_call(
        _moba_kernel,
        grid=(batch, npair, nb),
        in_specs=[qspec, kspec, vtspec],
        out_specs=qspec,
        out_shape=jax.ShapeDtypeStruct((batch * seq, MOBA_W), BF16),
        scratch_shapes=[pltpu.VMEM((nbp, LANES), F32), pltpu.VMEM((nbp, 2 * BLK), F32),
                        pltpu.VMEM((2 * BLK, 2 * BLK), F32), pltpu.VMEM((2 * BLK, 2 * BLK), F32),
                        pltpu.VMEM((2 * BLK, 2 * BLK), BF16), pltpu.VMEM((2 * BLK, 2 * BLK), BF16)],
        compiler_params=pltpu.CompilerParams(dimension_semantics=("arbitrary",) * 3,
                                             vmem_limit_bytes=VMEM_LIMIT),
        name="moba",
    )(mq, mk, mvt)


def _merge_kernel(x_ref, ret_ref, moba_ref, ga_ref, gb_ref, wr_ref, wm_ref, wo_ref, o_ref):
    a = _dot(ret_ref[...], wr_ref[...])
    b = _dot(moba_ref[...], wm_ref[...])
    mix = ga_ref[...].astype(F32) * a + gb_ref[...].astype(F32) * b
    o_ref[...] = x_ref[...] + _dot(mix.astype(BF16), wo_ref[...])


def _merge(x2, ret, moba, sga, sgb, w_ret_out, w_moba_out, w_o, *, tm):
    T = x2.shape[0]
    row = lambda w: pl.BlockSpec((tm, w), lambda i: (i, 0))
    return pl.pallas_call(
        _merge_kernel,
        grid=(T // tm,),
        in_specs=[row(D_MODEL), row(RET_V), row(MOBA_W), row(D_MODEL), row(D_MODEL),
                  _const_spec((RET_V, D_MODEL)), _const_spec((MOBA_W, D_MODEL)), _const_spec((D_MODEL, D_MODEL))],
        out_specs=row(D_MODEL),
        out_shape=jax.ShapeDtypeStruct((T, D_MODEL), F32),
        compiler_params=pltpu.CompilerParams(dimension_semantics=("arbitrary",), vmem_limit_bytes=VMEM_LIMIT),
        name="merge",
    )(x2, ret, moba, sga, sgb, w_ret_out.astype(BF16), w_moba_out.astype(BF16), w_o.astype(BF16))


def _ffn_kernel(x_ref, n2_ref, wg_ref, wu_ref, wd_ref, o_ref):
    x = x_ref[...]
    ms = jnp.mean(x * x, axis=-1, keepdims=True)
    h = (x * lax.rsqrt(ms + RMS_EPS) * n2_ref[...]).astype(BF16)
    g = _dot(h, wg_ref[...])
    u = _dot(h, wu_ref[...])
    act = (g * jax.nn.sigmoid(g) * u).astype(BF16)
    o_ref[...] = x + _dot(act, wd_ref[...])


def _ffn(x2, norm2_w, w_gate, w_up, w_down, *, tm):
    T = x2.shape[0]
    row = pl.BlockSpec((tm, D_MODEL), lambda i: (i, 0))
    return pl.pallas_call(
        _ffn_kernel,
        grid=(T // tm,),
        in_specs=[row, _const_spec((1, D_MODEL)), _const_spec((D_MODEL, FFN_HIDDEN)),
                  _const_spec((D_MODEL, FFN_HIDDEN)), _const_spec((FFN_HIDDEN, D_MODEL))],
        out_specs=row,
        out_shape=jax.ShapeDtypeStruct((T, D_MODEL), F32),
        compiler_params=pltpu.CompilerParams(dimension_semantics=("arbitrary",), vmem_limit_bytes=VMEM_LIMIT),
        name="ffn",
    )(x2, norm2_w.astype(F32)[None, :], w_gate.astype(BF16), w_up.astype(BF16), w_down.astype(BF16))


def kernel(x, norm1_w, w_in, q_norm_w, k_norm_w, w_ret_out, w_moba_out, w_o, norm2_w, w_ffn_gate, w_ffn_up, w_ffn_down):
    B, S, D = x.shape
    assert D == D_MODEL and S % MOBA_BLOCK == 0 and S % 512 == 0
    depth = norm1_w.shape[0]
    x2 = x.reshape(B * S, D)
    for l in range(depth):
        rq, rk, rv, rgs, mq, mk, mv, sga, sgb = _inproj(
            x2, norm1_w[l], w_in[l], q_norm_w[l], k_norm_w[l], seq=S, tm=512)
        ret = _retention(rq, rk, rv, rgs, batch=B, seq=S, rt=512)
        moba = _moba(mq, mk, mv, batch=B, seq=S)
        x2 = _merge(x2, ret, moba, sga, sgb, w_ret_out[l], w_moba_out[l], w_o[l], tm=512)
        x2 = _ffn(x2, norm2_w[l], w_ffn_gate[l], w_ffn_up[l], w_ffn_down[l], tm=512)
    return x2.reshape(B, S, D)
```

```python
import functools
import math

import numpy as np
import jax
import jax.numpy as jnp
from jax import lax
from jax.experimental import pallas as pl
from jax.experimental.pallas import tpu as pltpu

D_MODEL = 1024
RET_HEADS = 4
RET_DK = 128
RET_DV = 256
ROPE_BASE = 10000.0
MOBA_HEADS = 8
MOBA_DH = 64
MOBA_BLOCK = 256
MOBA_TOPK = 3
FFN_HIDDEN = 2816
RET_QK = RET_HEADS * RET_DK
RET_V = RET_HEADS * RET_DV
MOBA_W = MOBA_HEADS * MOBA_DH
IN_SPLITS = (RET_QK, RET_QK, RET_V, RET_V, MOBA_W, MOBA_W, MOBA_W, D_MODEL, D_MODEL)
IN_COLS = sum(IN_SPLITS)
RMS_EPS = 1e-6
GN_EPS = 1e-5
NEG = -1e30

LANES = 128
VMEM_LIMIT = 52 * 1024 * 1024
RET_CHUNK = 256

F32 = jnp.float32
BF16 = jnp.bfloat16


def _dot(a, b):
    return jnp.dot(a, b, preferred_element_type=F32)


def _dot_nt(a, b):
    return lax.dot_general(a, b, (((1,), (1,)), ((), ())), preferred_element_type=F32)


def _dot_tn(a, b):
    return lax.dot_general(a, b, (((0,), (0,)), ((), ())), preferred_element_type=F32)


def _const_spec(shape):
    nd = len(shape)
    return pl.BlockSpec(shape, lambda *_: (0,) * nd, pipeline_mode=pl.Buffered(1))


def _inproj_kernel(x_ref, n1_ref, w_ref, wvt_ref, qn_ref, kn_ref, inv_ref, grp_ref,
                   rq_ref, rk_ref, rv_ref, rg_ref, mq_ref, mk_ref, mvt_ref, ga_ref, gb_ref,
                   *, tm, seq):
    i = pl.program_id(0)
    x = x_ref[...]
    ms = jnp.mean(x * x, axis=-1, keepdims=True)
    h = (x * lax.rsqrt(ms + RMS_EPS) * n1_ref[...]).astype(BF16)

    offs = np.cumsum((0,) + IN_SPLITS)

    def proj(n):
        return _dot(h, w_ref[:, int(offs[n]):int(offs[n + 1])])

    pos0 = (i * tm) % seq
    pos = (pos0 + lax.broadcasted_iota(jnp.int32, (tm, LANES), 0)).astype(F32)
    lane = lax.broadcasted_iota(jnp.int32, (tm, LANES), 1)
    ang = pos * inv_ref[...]
    cos = jnp.cos(ang)
    sin = jnp.where(lane < RET_DK // 2, -jnp.sin(ang), jnp.sin(ang))

    def rotary_store(p, out_ref, scale):
        for hd in range(RET_HEADS):
            xh = p[:, hd * RET_DK:(hd + 1) * RET_DK]
            y = xh * cos + pltpu.roll(xh, RET_DK // 2, 1) * sin
            if scale is not None:
                y = y * scale
            out_ref[:, hd * RET_DK:(hd + 1) * RET_DK] = y.astype(BF16)

    rotary_store(proj(0), rq_ref, None)
    rotary_store(proj(1), rk_ref, RET_DK ** -0.5)
    rv_ref[...] = proj(2).astype(BF16)
    rg = proj(3)
    rg_ref[...] = (rg * jax.nn.sigmoid(rg)).astype(BF16)

    def head_rms_store(p, w_row_ref, out_ref, scale):
        sq = p * p
        hi = sq.astype(BF16)
        lo = (sq - hi.astype(F32)).astype(BF16)
        msq = (_dot(hi, grp_ref[...]) + _dot(lo, grp_ref[...])) * (1.0 / MOBA_DH)
        y = p * lax.rsqrt(msq + RMS_EPS) * w_row_ref[...]
        out_ref[...] = (y if scale is None else y * scale).astype(BF16)

    head_rms_store(proj(4), qn_ref, mq_ref, MOBA_DH ** -0.5 * math.log2(math.e))
    head_rms_store(proj(5), kn_ref, mk_ref, None)
    mvt = _dot_nt(wvt_ref[...], h).astype(BF16)
    for pair in range(MOBA_W // LANES):
        for c in range(tm // MOBA_BLOCK):
            mvt_ref[0, pair, c] = mvt[pair * LANES:(pair + 1) * LANES, c * MOBA_BLOCK:(c + 1) * MOBA_BLOCK]
    ga_ref[...] = jax.nn.sigmoid(proj(7)).astype(BF16)
    gb_ref[...] = jax.nn.sigmoid(proj(8)).astype(BF16)


def _inproj(x2, norm1_w, w_in, q_norm_w, k_norm_w, *, seq, tm):
    T = x2.shape[0]
    half = RET_DK // 2
    inv = ROPE_BASE ** (-jnp.arange(half, dtype=F32) / half)
    inv2 = jnp.concatenate([inv, inv])[None, :]
    grp = jnp.asarray(np.kron(np.eye(MOBA_HEADS), np.ones((MOBA_DH, MOBA_DH))), BF16)
    qn = jnp.tile(q_norm_w.astype(F32), MOBA_HEADS)[None, :]
    kn = jnp.tile(k_norm_w.astype(F32), MOBA_HEADS)[None, :]
    offs = np.cumsum((0,) + IN_SPLITS)
    w_mv_t = w_in[:, int(offs[6]):int(offs[7])].T.astype(BF16)
    nt = seq // tm
    npair = MOBA_W // LANES
    nb = seq // MOBA_BLOCK
    row = lambda w: pl.BlockSpec((tm, w), lambda i: (i, 0))
    row_out = lambda w: (row(w), jax.ShapeDtypeStruct((T, w), BF16))
    mvt_out = (pl.BlockSpec((1, npair, tm // MOBA_BLOCK, LANES, MOBA_BLOCK), lambda i: (i // nt, 0, i % nt, 0, 0)),
               jax.ShapeDtypeStruct((T // seq, npair, nb, LANES, MOBA_BLOCK), BF16))
    outs = [row_out(RET_QK), row_out(RET_QK), row_out(RET_V), row_out(RET_V), row_out(MOBA_W), row_out(MOBA_W),
            mvt_out, row_out(D_MODEL), row_out(D_MODEL)]
    return pl.pallas_call(
        functools.partial(_inproj_kernel, tm=tm, seq=seq),
        grid=(T // tm,),
        in_specs=[row(D_MODEL), _const_spec((1, D_MODEL)), _const_spec((D_MODEL, IN_COLS)),
                  _const_spec((MOBA_W, D_MODEL)),
                  _const_spec((1, MOBA_W)), _const_spec((1, MOBA_W)), _const_spec((1, LANES)),
                  _const_spec((MOBA_W, MOBA_W))],
        out_specs=[o[0] for o in outs],
        out_shape=[o[1] for o in outs],
        compiler_params=pltpu.CompilerParams(dimension_semantics=("arbitrary",), vmem_limit_bytes=VMEM_LIMIT),
        name="inproj",
    )(x2, norm1_w.astype(F32)[None, :], w_in.astype(BF16), w_mv_t, qn, kn, inv2, grp)


def _retention_kernel(q_ref, k_ref, v_ref, g_ref, dmat_ref, xi_ref, zeta_ref, o_ref, state_ref,
                      *, rt, chunk, g_chunk):
    @pl.when(pl.program_id(1) == 0)
    def _():
        state_ref[...] = jnp.zeros_like(state_ref)

    for c in range(rt // chunk):
        rows = slice(c * chunk, (c + 1) * chunk)
        for hd in range(RET_HEADS):
            kcols = slice(hd * RET_DK, (hd + 1) * RET_DK)
            vcols = slice(hd * RET_DV, (hd + 1) * RET_DV)
            q = q_ref[rows, kcols]
            k = k_ref[rows, kcols]
            v = v_ref[rows, vcols]
            state = state_ref[hd]
            scores = _dot_nt(q, k) * dmat_ref[hd]
            q_dec = (q.astype(F32) * xi_ref[hd]).astype(BF16)
            o = _dot(scores.astype(BF16), v) + _dot(q_dec, state.astype(BF16))
            k_dec = (k.astype(F32) * zeta_ref[hd]).astype(BF16)
            state_ref[hd] = g_chunk[hd] * state + _dot_tn(k_dec, v)
            mu = jnp.mean(o, axis=-1, keepdims=True)
            d = o - mu
            var = jnp.mean(d * d, axis=-1, keepdims=True)
            y = d * lax.rsqrt(var + GN_EPS)
            o_ref[rows, vcols] = (g_ref[rows, vcols].astype(F32) * y).astype(BF16)


def _retention(rq, rk, rv, rgs, *, batch, seq, rt):
    C = RET_CHUNK
    lg = np.log1p(-np.exp2(-5.0 - np.arange(RET_HEADS, dtype=np.float64)))
    idx = np.arange(C, dtype=np.float64)
    diff = idx[:, None] - idx[None, :]
    dmat = np.where(diff >= 0, np.exp(np.maximum(diff, 0.0)[None] * lg[:, None, None]), 0.0)
    xi = np.broadcast_to(np.exp((idx + 1.0)[None, :] * lg[:, None])[:, :, None], (RET_HEADS, C, LANES))
    zeta = np.broadcast_to(np.exp((C - 1.0 - idx)[None, :] * lg[:, None])[:, :, None], (RET_HEADS, C, LANES))
    g_chunk = tuple(float(v) for v in np.exp(C * lg))
    nt = seq // rt
    row = lambda w: pl.BlockSpec((rt, w), lambda b, j: (b * nt + j, 0))
    return pl.pallas_call(
        functools.partial(_retention_kernel, rt=rt, chunk=C, g_chunk=g_chunk),
        grid=(batch, nt),
        in_specs=[row(RET_QK), row(RET_QK), row(RET_V), row(RET_V),
                  _const_spec((RET_HEADS, C, C)), _const_spec((RET_HEADS, C, LANES)),
                  _const_spec((RET_HEADS, C, LANES))],
        out_specs=row(RET_V),
        out_shape=jax.ShapeDtypeStruct((batch * seq, RET_V), BF16),
        scratch_shapes=[pltpu.VMEM((RET_HEADS, RET_DK, RET_DV), F32)],
        compiler_params=pltpu.CompilerParams(dimension_semantics=("arbitrary", "arbitrary"),
                                             vmem_limit_bytes=VMEM_LIMIT),
        name="retention",
    )(rq, rk, rv, rgs, jnp.asarray(dmat, F32), jnp.asarray(xi, F32), jnp.asarray(zeta, F32))


def _moba_kernel(tile_tab_ref, stage_tab_ref, q_ref, k_ref, vt_ref, o_ref,
                 kbar_ref, bias_ref, qcat_ref, m_ref, r_ref, l_ref, acc_ref,
                 s0_ref, s1_ref, s2_ref, s3_ref, p0_ref, p1_ref, *, n_items):
    BLK = MOBA_BLOCK
    DH = MOBA_DH
    STG = 2 * BLK
    nb = vt_ref.shape[2]
    nbp = kbar_ref.shape[0]
    s_bufs = (s0_ref, s1_ref, s2_ref, s3_ref)
    p_bufs = (p0_ref, p1_ref)
    lane = lax.broadcasted_iota(jnp.int32, (1, LANES), 1)

    def q_cat_body(i, c):
        q_pair = q_ref[pl.ds(pl.multiple_of(i * BLK, BLK), BLK), :]
        zero = jnp.zeros_like(q_pair)
        qcat_ref[i, :BLK, :] = jnp.where(lane < DH, q_pair, zero)
        qcat_ref[i, BLK:, :] = jnp.where(lane >= DH, q_pair, zero)
        return c

    lax.fori_loop(0, nb, q_cat_body, 0)

    def q_cat_of(i):
        return qcat_ref[i]

    kbar_ref[...] = jnp.zeros_like(kbar_ref)

    def kbar_body(n, c):
        kb = k_ref[pl.ds(pl.multiple_of(n * BLK, BLK), BLK), :]
        kbar_ref[pl.ds(n, 1), :] = jnp.mean(kb.astype(F32), axis=0, keepdims=True)
        return c

    lax.fori_loop(0, nb, kbar_body, 0)

    kbar = kbar_ref[...]
    kbar_hi = kbar.astype(BF16)
    kbar_lo = (kbar - kbar_hi.astype(F32)).astype(BF16)
    blk_idx = lax.broadcasted_iota(jnp.int32, (nbp, 2 * BLK), 0)

    def select_blocks(i):
        q_cat = q_cat_of(i)
        gate = _dot_nt(kbar_hi, q_cat) + _dot_nt(kbar_lo, q_cat)
        past = blk_idx < i
        gate = jnp.where(past, gate, NEG)
        sel = jnp.zeros((nbp, 2 * BLK), jnp.bool_)
        for _ in range(MOBA_TOPK):
            best = jnp.max(gate, axis=0, keepdims=True)
            first = jnp.min(jnp.where(gate == best, blk_idx, nbp), axis=0, keepdims=True)
            pick = blk_idx == first
            sel = sel | (pick & past)
            gate = jnp.where(pick, -3e38, gate)
        bias_ref[i] = jnp.where(sel, 0.0, NEG)

    def select_body(u, c):
        select_blocks(2 * u)
        select_blocks(2 * u + 1)
        return c

    lax.fori_loop(0, nb // 2, select_body, 0)

    m_ref[...] = jnp.full(m_ref.shape, NEG, F32)
    r_ref[...] = jnp.full(r_ref.shape, NEG, F32)
    l_ref[...] = jnp.zeros_like(l_ref)
    acc_ref[...] = jnp.zeros_like(acc_ref)

    def scores(i, st, dst_ref):
        ks = k_ref[pl.ds(pl.multiple_of(st * STG, STG), STG), :]
        dst_ref[...] = _dot_nt(ks, q_cat_of(i))

    def values(st, p_ref):
        vt = jnp.concatenate([vt_ref[0, 0, 2 * st], vt_ref[0, 0, 2 * st + 1]], axis=1)
        return _dot(jnp.concatenate([vt, jnp.ones((16, STG), BF16)], axis=0), p_ref[...])

    def fold(pv, m_pv, i):
        scale = jnp.exp2(r_ref[i] - m_pv)
        r_ref[i] = m_pv
        l_ref[i] = scale * l_ref[i] + pv[2 * DH:2 * DH + 1, :]
        acc_ref[i, :DH, :] = scale[:, :BLK] * acc_ref[i, :DH, :] + pv[:DH, :BLK]
        acc_ref[i, DH:, :] = scale[:, BLK:] * acc_ref[i, DH:, :] + pv[DH:2 * DH, BLK:]

    def probs_past(i, st, s_ref, p_ref, tok):
        b0 = bias_ref[i, pl.ds(2 * st, 1), :]
        b1 = bias_ref[i, pl.ds(2 * st + 1, 1), :]
        s0 = s_ref[:BLK, :]
        s1 = s_ref[BLK:, :]
        m_new = jnp.maximum(m_ref[i], jnp.maximum(jnp.max(s0, axis=0, keepdims=True) + b0,
                                                  jnp.max(s1, axis=0, keepdims=True) + b1))
        m_ref[i] = m_new
        m_new = m_new + tok
        p_ref[:BLK, :] = jnp.exp2(s0 - (m_new - b0)).astype(BF16)
        p_ref[BLK:, :] = jnp.exp2(s1 - (m_new - b1)).astype(BF16)
        return m_new

    key_idx = lax.broadcasted_iota(jnp.int32, (BLK, 2 * BLK), 0)
    qry_idx = lax.broadcasted_iota(jnp.int32, (BLK, 2 * BLK), 1) & (BLK - 1)
    causal = key_idx <= qry_idx

    def probs_diag_even(i, st, s_ref, p_ref, tok):
        s0 = jnp.where(causal, s_ref[:BLK, :], NEG)
        m_new = jnp.maximum(m_ref[i], jnp.max(s0, axis=0, keepdims=True))
        m_ref[i] = m_new
        m_new = m_new + tok
        p_ref[:BLK, :] = jnp.exp2(s0 - m_new).astype(BF16)
        p_ref[BLK:, :] = jnp.zeros((BLK, 2 * BLK), BF16)
        return m_new

    def probs_diag_odd(i, st, s_ref, p_ref, tok):
        b0 = bias_ref[i, pl.ds(2 * st, 1), :]
        s0 = s_ref[:BLK, :]
        s1 = jnp.where(causal, s_ref[BLK:, :], NEG)
        m_new = jnp.maximum(m_ref[i], jnp.maximum(jnp.max(s0, axis=0, keepdims=True) + b0,
                                                  jnp.max(s1, axis=0, keepdims=True)))
        m_ref[i] = m_new
        m_new = m_new + tok
        p_ref[:BLK, :] = jnp.exp2(s0 - (m_new - b0)).astype(BF16)
        p_ref[BLK:, :] = jnp.exp2(s1 - m_new).astype(BF16)
        return m_new

    def run_items(count, item, probs):
        def clamped(t):
            return item(jnp.clip(t, 0, count - 1))

        scores(*clamped(0), s_bufs[0])
        scores(*clamped(1), s_bufs[1])

        def slot(t, k, carry):
            m_prev, tok = carry
            scores(*clamped(t + 2), s_bufs[(k + 2) % 4])
            i_prev, st_prev = clamped(t - 1)
            pv = values(st_prev, p_bufs[(k - 1) % 2])
            i, st = clamped(t)
            m_t = probs(i, st, s_bufs[k], p_bufs[k % 2], tok)
            fold(pv, m_prev, i_prev)
            return m_t, pv[2 * DH:2 * DH + 1, :] * 0.0

        def body(it, carry):
            for k in range(4):
                carry = slot(4 * it + k, k, carry)
            return carry

        m_last, _ = lax.fori_loop(0, count // 4, body, (r_ref[clamped(0)[0]], jnp.zeros((1, 2 * BLK), F32)))
        i_last, st_last = clamped(count - 1)
        pv_last = values(st_last, p_bufs[(count - 1) % 2])
        fold(pv_last, m_last, i_last)
        p_bufs[1][...] = jnp.broadcast_to((pv_last[0:1, :] * 0.0).astype(BF16), p_bufs[1].shape)

    p_bufs[1][...] = jnp.zeros_like(p_bufs[1])
    run_items(nb // 2, lambda u: (2 * u, u), probs_diag_even)
    run_items(nb // 2, lambda u: (2 * u + 1, u), probs_diag_odd)
    run_items(n_items, lambda t: (tile_tab_ref[t], stage_tab_ref[t]), probs_past)

    def finish(i, c):
        l = l_ref[i]
        acc = acc_ref[i]
        out_t = jnp.concatenate([acc[:DH] / l[:, :BLK], acc[DH:] / l[:, BLK:]], axis=0)
        o_ref[pl.ds(pl.multiple_of(i * BLK, BLK), BLK), :] = out_t.T.astype(BF16)
        return c

    lax.fori_loop(0, nb, finish, 0)


def _moba(mq, mk, mvt, *, batch, seq):
    BLK = MOBA_BLOCK
    nb = seq // BLK
    assert nb % 8 == 0
    npair = MOBA_W // LANES
    n_stage = nb // 2
    items = [(i, st) for st in range(n_stage) for i in range(2 * st + 2, nb)]
    assert len(items) % 4 == 0
    tile_tab = jnp.asarray([i for i, _ in items], jnp.int32)
    stage_tab = jnp.asarray([st for _, st in items], jnp.int32)
    qkspec = pl.BlockSpec((seq, LANES), lambda b, hp, *_: (b, hp))
    vtspec = pl.BlockSpec((1, 1, nb, LANES, BLK), lambda b, hp, *_: (b, hp, 0, 0, 0))
    stat = pltpu.VMEM((nb, 1, 2 * BLK), F32)
    sbuf = pltpu.VMEM((2 * BLK, 2 * BLK), F32)
    pbuf = pltpu.VMEM((2 * BLK, 2 * BLK), BF16)
    return pl.pallas_call(
        functools.partial(_moba_kernel, n_items=len(items)),
        grid_spec=pltpu.PrefetchScalarGridSpec(
            num_scalar_prefetch=2,
            grid=(batch, npair),
            in_specs=[qkspec, qkspec, vtspec],
            out_specs=qkspec,
            scratch_shapes=[pltpu.VMEM((nb, LANES), F32), pltpu.VMEM((nb, nb, 2 * BLK), F32),
                            pltpu.VMEM((nb, 2 * BLK, LANES), BF16),
                            stat, stat, stat, pltpu.VMEM((nb, LANES, BLK), F32),
                            sbuf, sbuf, sbuf, sbuf, pbuf, pbuf]),
        out_shape=jax.ShapeDtypeStruct((batch * seq, MOBA_W), BF16),
        compiler_params=pltpu.CompilerParams(dimension_semantics=("arbitrary",) * 2,
                                             vmem_limit_bytes=VMEM_LIMIT),
        name="moba",
    )(tile_tab, stage_tab, mq, mk, mvt)


def _merge_kernel(x_ref, ret_ref, moba_ref, ga_ref, gb_ref, wr_ref, wm_ref, wo_ref, o_ref):
    a = _dot(ret_ref[...], wr_ref[...])
    b = _dot(moba_ref[...], wm_ref[...])
    mix = ga_ref[...].astype(F32) * a + gb_ref[...].astype(F32) * b
    o_ref[...] = x_ref[...] + _dot(mix.astype(BF16), wo_ref[...])


def _merge(x2, ret, moba, sga, sgb, w_ret_out, w_moba_out, w_o, *, tm):
    T = x2.shape[0]
    row = lambda w: pl.BlockSpec((tm, w), lambda i: (i, 0))
    return pl.pallas_call(
        _merge_kernel,
        grid=(T // tm,),
        in_specs=[row(D_MODEL), row(RET_V), row(MOBA_W), row(D_MODEL), row(D_MODEL),
                  _const_spec((RET_V, D_MODEL)), _const_spec((MOBA_W, D_MODEL)), _const_spec((D_MODEL, D_MODEL))],
        out_specs=row(D_MODEL),
        out_shape=jax.ShapeDtypeStruct((T, D_MODEL), F32),
        compiler_params=pltpu.CompilerParams(dimension_semantics=("arbitrary",), vmem_limit_bytes=VMEM_LIMIT),
        name="merge",
    )(x2, ret, moba, sga, sgb, w_ret_out.astype(BF16), w_moba_out.astype(BF16), w_o.astype(BF16))


def _ffn_kernel(x_ref, n2_ref, wg_ref, wu_ref, wd_ref, o_ref):
    x = x_ref[...]
    ms = jnp.mean(x * x, axis=-1, keepdims=True)
    h = (x * lax.rsqrt(ms + RMS_EPS) * n2_ref[...]).astype(BF16)
    g = _dot(h, wg_ref[...])
    u = _dot(h, wu_ref[...])
    act = (g * jax.nn.sigmoid(g) * u).astype(BF16)
    o_ref[...] = x + _dot(act, wd_ref[...])


def _ffn(x2, norm2_w, w_gate, w_up, w_down, *, tm):
    T = x2.shape[0]
    row = pl.BlockSpec((tm, D_MODEL), lambda i: (i, 0))
    return pl.pallas_call(
        _ffn_kernel,
        grid=(T // tm,),
        in_specs=[row, _const_spec((1, D_MODEL)), _const_spec((D_MODEL, FFN_HIDDEN)),
                  _const_spec((D_MODEL, FFN_HIDDEN)), _const_spec((FFN_HIDDEN, D_MODEL))],
        out_specs=row,
        out_shape=jax.ShapeDtypeStruct((T, D_MODEL), F32),
        compiler_params=pltpu.CompilerParams(dimension_semantics=("arbitrary",), vmem_limit_bytes=VMEM_LIMIT),
        name="ffn",
    )(x2, norm2_w.astype(F32)[None, :], w_gate.astype(BF16), w_up.astype(BF16), w_down.astype(BF16))


def kernel(x, norm1_w, w_in, q_norm_w, k_norm_w, w_ret_out, w_moba_out, w_o, norm2_w, w_ffn_gate, w_ffn_up, w_ffn_down):
    B, S, D = x.shape
    assert D == D_MODEL and S % MOBA_BLOCK == 0 and S % 512 == 0
    depth = norm1_w.shape[0]
    x2 = x.reshape(B * S, D)
    for l in range(depth):
        rq, rk, rv, rgs, mq, mk, mvt, sga, sgb = _inproj(
            x2, norm1_w[l], w_in[l], q_norm_w[l], k_norm_w[l], seq=S, tm=512)
        ret = _retention(rq, rk, rv, rgs, batch=B, seq=S, rt=512)
        moba = _moba(mq, mk, mvt, batch=B, seq=S)
        x2 = _merge(x2, ret, moba, sga, sgb, w_ret_out[l], w_moba_out[l], w_o[l], tm=512)
        x2 = _ffn(x2, norm2_w[l], w_ffn_gate[l], w_ffn_up[l], w_ffn_down[l], tm=512)
    return x2.reshape(B, S, D)
```

```python
import functools
import math

import numpy as np
import jax
import jax.numpy as jnp
from jax import lax
from jax.experimental import pallas as pl
from jax.experimental.pallas import tpu as pltpu

D_MODEL = 1024
RET_HEADS = 4
RET_DK = 128
RET_DV = 256
ROPE_BASE = 10000.0
MOBA_HEADS = 8
MOBA_DH = 64
MOBA_BLOCK = 256
MOBA_TOPK = 3
FFN_HIDDEN = 2816
RET_QK = RET_HEADS * RET_DK
RET_V = RET_HEADS * RET_DV
MOBA_W = MOBA_HEADS * MOBA_DH
IN_SPLITS = (RET_QK, RET_QK, RET_V, RET_V, MOBA_W, MOBA_W, MOBA_W, D_MODEL, D_MODEL)
IN_COLS = sum(IN_SPLITS)
RMS_EPS = 1e-6
GN_EPS = 1e-5
NEG = -1e30

LANES = 128
VMEM_LIMIT = 52 * 1024 * 1024
RET_CHUNK = 256

F32 = jnp.float32
BF16 = jnp.bfloat16


def _dot(a, b):
    return jnp.dot(a, b, preferred_element_type=F32)


def _dot_nt(a, b):
    return lax.dot_general(a, b, (((1,), (1,)), ((), ())), preferred_element_type=F32)


def _dot_tn(a, b):
    return lax.dot_general(a, b, (((0,), (0,)), ((), ())), preferred_element_type=F32)


def _const_spec(shape):
    nd = len(shape)
    return pl.BlockSpec(shape, lambda *_: (0,) * nd, pipeline_mode=pl.Buffered(1))


def _inproj_kernel(x_ref, n1_ref, w_ref, wvt_ref, qn_ref, kn_ref, inv_ref,
                   rq_ref, rk_ref, rv_ref, rg_ref, mq_ref, mk_ref, mvt_ref, ga_ref, gb_ref,
                   *, tm, seq):
    i = pl.program_id(0)
    x = x_ref[...]
    ms = jnp.mean(x * x, axis=-1, keepdims=True)
    h = (x * lax.rsqrt(ms + RMS_EPS) * n1_ref[...]).astype(BF16)

    offs = np.cumsum((0,) + IN_SPLITS)

    def proj(n):
        return _dot(h, w_ref[:, int(offs[n]):int(offs[n + 1])])

    pos0 = (i * tm) % seq
    pos = (pos0 + lax.broadcasted_iota(jnp.int32, (tm, LANES), 0)).astype(F32)
    lane = lax.broadcasted_iota(jnp.int32, (tm, LANES), 1)
    ang = pos * inv_ref[...]
    cos = jnp.cos(ang)
    sin = jnp.where(lane < RET_DK // 2, -jnp.sin(ang), jnp.sin(ang))

    def rotary_store(p, out_ref, scale):
        for hd in range(RET_HEADS):
            xh = p[:, hd * RET_DK:(hd + 1) * RET_DK]
            y = xh * cos + pltpu.roll(xh, RET_DK // 2, 1) * sin
            if scale is not None:
                y = y * scale
            out_ref[:, hd * RET_DK:(hd + 1) * RET_DK] = y.astype(BF16)

    rotary_store(proj(0), rq_ref, None)
    rotary_store(proj(1), rk_ref, RET_DK ** -0.5)
    rv_ref[...] = proj(2).astype(BF16)
    rg = proj(3)
    rg_ref[...] = (rg * jax.nn.sigmoid(rg)).astype(BF16)

    low_half = lane < MOBA_DH

    def head_rms_store(p, w_row_ref, out_ref, scale):
        for g in range(MOBA_W // LANES):
            cols = slice(g * LANES, (g + 1) * LANES)
            ph = p[:, cols]
            sq = ph * ph
            lo = jnp.sum(jnp.where(low_half, sq, 0.0), axis=-1, keepdims=True)
            hi = jnp.sum(jnp.where(low_half, 0.0, sq), axis=-1, keepdims=True)
            msq = jnp.where(low_half, lo, hi) * (1.0 / MOBA_DH)
            y = ph * lax.rsqrt(msq + RMS_EPS) * w_row_ref[:, cols]
            out_ref[:, cols] = (y if scale is None else y * scale).astype(BF16)

    head_rms_store(proj(4), qn_ref, mq_ref, MOBA_DH ** -0.5 * math.log2(math.e))
    head_rms_store(proj(5), kn_ref, mk_ref, None)
    mvt = _dot_nt(wvt_ref[...], h).astype(BF16)
    for pair in range(MOBA_W // LANES):
        for c in range(tm // MOBA_BLOCK):
            mvt_ref[0, pair, c] = mvt[pair * LANES:(pair + 1) * LANES, c * MOBA_BLOCK:(c + 1) * MOBA_BLOCK]
    ga_ref[...] = jax.nn.sigmoid(proj(7)).astype(BF16)
    gb_ref[...] = jax.nn.sigmoid(proj(8)).astype(BF16)


def _inproj(x2, norm1_w, w_in, q_norm_w, k_norm_w, *, seq, tm):
    T = x2.shape[0]
    half = RET_DK // 2
    inv = ROPE_BASE ** (-jnp.arange(half, dtype=F32) / half)
    inv2 = jnp.concatenate([inv, inv])[None, :]
    qn = jnp.tile(q_norm_w.astype(F32), MOBA_HEADS)[None, :]
    kn = jnp.tile(k_norm_w.astype(F32), MOBA_HEADS)[None, :]
    offs = np.cumsum((0,) + IN_SPLITS)
    w_mv_t = w_in[:, int(offs[6]):int(offs[7])].T.astype(BF16)
    nt = seq // tm
    npair = MOBA_W // LANES
    nb = seq // MOBA_BLOCK
    row = lambda w: pl.BlockSpec((tm, w), lambda i: (i, 0))
    row_out = lambda w: (row(w), jax.ShapeDtypeStruct((T, w), BF16))
    mvt_out = (pl.BlockSpec((1, npair, tm // MOBA_BLOCK, LANES, MOBA_BLOCK), lambda i: (i // nt, 0, i % nt, 0, 0)),
               jax.ShapeDtypeStruct((T // seq, npair, nb, LANES, MOBA_BLOCK), BF16))
    outs = [row_out(RET_QK), row_out(RET_QK), row_out(RET_V), row_out(RET_V), row_out(MOBA_W), row_out(MOBA_W),
            mvt_out, row_out(D_MODEL), row_out(D_MODEL)]
    return pl.pallas_call(
        functools.partial(_inproj_kernel, tm=tm, seq=seq),
        grid=(T // tm,),
        in_specs=[row(D_MODEL), _const_spec((1, D_MODEL)), _const_spec((D_MODEL, IN_COLS)),
                  _const_spec((MOBA_W, D_MODEL)),
                  _const_spec((1, MOBA_W)), _const_spec((1, MOBA_W)), _const_spec((1, LANES))],
        out_specs=[o[0] for o in outs],
        out_shape=[o[1] for o in outs],
        compiler_params=pltpu.CompilerParams(dimension_semantics=("arbitrary",), vmem_limit_bytes=VMEM_LIMIT),
        name="inproj",
    )(x2, norm1_w.astype(F32)[None, :], w_in.astype(BF16), w_mv_t, qn, kn, inv2)


def _retention_kernel(q_ref, k_ref, v_ref, g_ref, dmat_ref, xi_ref, zeta_ref, o_ref, state_ref,
                      *, rt, chunk, g_chunk):
    @pl.when(pl.program_id(1) == 0)
    def _():
        state_ref[...] = jnp.zeros_like(state_ref)

    for c in range(rt // chunk):
        rows = slice(c * chunk, (c + 1) * chunk)
        for hd in range(RET_HEADS):
            kcols = slice(hd * RET_DK, (hd + 1) * RET_DK)
            vcols = slice(hd * RET_DV, (hd + 1) * RET_DV)
            q = q_ref[rows, kcols]
            k = k_ref[rows, kcols]
            v = v_ref[rows, vcols]
            state = state_ref[hd]
            scores = _dot_nt(q, k) * dmat_ref[hd]
            q_dec = (q.astype(F32) * xi_ref[hd]).astype(BF16)
            o = _dot(scores.astype(BF16), v) + _dot(q_dec, state.astype(BF16))
            k_dec = (k.astype(F32) * zeta_ref[hd]).astype(BF16)
            state_ref[hd] = g_chunk[hd] * state + _dot_tn(k_dec, v)
            mu = jnp.mean(o, axis=-1, keepdims=True)
            d = o - mu
            var = jnp.mean(d * d, axis=-1, keepdims=True)
            y = d * lax.rsqrt(var + GN_EPS)
            o_ref[rows, vcols] = (g_ref[rows, vcols].astype(F32) * y).astype(BF16)


def _retention(rq, rk, rv, rgs, *, batch, seq, rt):
    C = RET_CHUNK
    lg = np.log1p(-np.exp2(-5.0 - np.arange(RET_HEADS, dtype=np.float64)))
    idx = np.arange(C, dtype=np.float64)
    diff = idx[:, None] - idx[None, :]
    dmat = np.where(diff >= 0, np.exp(np.maximum(diff, 0.0)[None] * lg[:, None, None]), 0.0)
    xi = np.broadcast_to(np.exp((idx + 1.0)[None, :] * lg[:, None])[:, :, None], (RET_HEADS, C, LANES))
    zeta = np.broadcast_to(np.exp((C - 1.0 - idx)[None, :] * lg[:, None])[:, :, None], (RET_HEADS, C, LANES))
    g_chunk = tuple(float(v) for v in np.exp(C * lg))
    nt = seq // rt
    row = lambda w: pl.BlockSpec((rt, w), lambda b, j: (b * nt + j, 0))
    return pl.pallas_call(
        functools.partial(_retention_kernel, rt=rt, chunk=C, g_chunk=g_chunk),
        grid=(batch, nt),
        in_specs=[row(RET_QK), row(RET_QK), row(RET_V), row(RET_V),
                  _const_spec((RET_HEADS, C, C)), _const_spec((RET_HEADS, C, LANES)),
                  _const_spec((RET_HEADS, C, LANES))],
        out_specs=row(RET_V),
        out_shape=jax.ShapeDtypeStruct((batch * seq, RET_V), BF16),
        scratch_shapes=[pltpu.VMEM((RET_HEADS, RET_DK, RET_DV), F32)],
        compiler_params=pltpu.CompilerParams(dimension_semantics=("arbitrary", "arbitrary"),
                                             vmem_limit_bytes=VMEM_LIMIT),
        name="retention",
    )(rq, rk, rv, rgs, jnp.asarray(dmat, F32), jnp.asarray(xi, F32), jnp.asarray(zeta, F32))


def _moba_kernel(tile_tab_ref, stage_tab_ref, q_ref, k_ref, vt_ref, o_ref,
                 kbar_ref, bias_ref, qcat_ref, m_ref, r_ref, l_ref, acc_ref,
                 s0_ref, s1_ref, s2_ref, s3_ref, p0_ref, p1_ref, p2_ref, p3_ref, *, n_items):
    BLK = MOBA_BLOCK
    DH = MOBA_DH
    STG = 2 * BLK
    nb = vt_ref.shape[2]
    nbp = kbar_ref.shape[0]
    s_bufs = (s0_ref, s1_ref, s2_ref, s3_ref)
    p_bufs = (p0_ref, p1_ref, p2_ref, p3_ref)
    lane = lax.broadcasted_iota(jnp.int32, (1, LANES), 1)

    def q_cat_body(i, c):
        q_pair = q_ref[pl.ds(pl.multiple_of(i * BLK, BLK), BLK), :]
        zero = jnp.zeros_like(q_pair)
        qcat_ref[i, :BLK, :] = jnp.where(lane < DH, q_pair, zero)
        qcat_ref[i, BLK:, :] = jnp.where(lane >= DH, q_pair, zero)
        return c

    lax.fori_loop(0, nb, q_cat_body, 0)

    def q_cat_of(i):
        return qcat_ref[i]

    kbar_ref[...] = jnp.zeros_like(kbar_ref)

    def kbar_body(n, c):
        kb = k_ref[pl.ds(pl.multiple_of(n * BLK, BLK), BLK), :]
        kbar_ref[pl.ds(n, 1), :] = jnp.mean(kb.astype(F32), axis=0, keepdims=True)
        return c

    lax.fori_loop(0, nb, kbar_body, 0)

    kbar = kbar_ref[...]
    kbar_hi = kbar.astype(BF16)
    kbar_lo = (kbar - kbar_hi.astype(F32)).astype(BF16)
    blk_idx = lax.broadcasted_iota(jnp.int32, (nbp, 2 * BLK), 0)

    def select_blocks(i):
        q_cat = q_cat_of(i)
        gate = _dot_nt(kbar_hi, q_cat) + _dot_nt(kbar_lo, q_cat)
        past = blk_idx < i
        gate = jnp.where(past, gate, NEG)
        sel = jnp.zeros((nbp, 2 * BLK), jnp.bool_)
        for _ in range(MOBA_TOPK):
            best = jnp.max(gate, axis=0, keepdims=True)
            first = jnp.min(jnp.where(gate == best, blk_idx, nbp), axis=0, keepdims=True)
            pick = blk_idx == first
            sel = sel | (pick & past)
            gate = jnp.where(pick, -3e38, gate)
        bias_ref[i] = jnp.where(sel, 0.0, NEG)

    def select_body(u, c):
        select_blocks(2 * u)
        select_blocks(2 * u + 1)
        return c

    lax.fori_loop(0, nb // 2, select_body, 0)

    m_ref[...] = jnp.full(m_ref.shape, NEG, F32)
    r_ref[...] = jnp.full(r_ref.shape, NEG, F32)
    l_ref[...] = jnp.zeros_like(l_ref)
    acc_ref[...] = jnp.zeros_like(acc_ref)

    def scores(i, st, dst_ref):
        ks = k_ref[pl.ds(pl.multiple_of(st * STG, STG), STG), :]
        dst_ref[...] = _dot_nt(ks, q_cat_of(i))

    def values(st, p_ref):
        vt = jnp.concatenate([vt_ref[0, 0, 2 * st], vt_ref[0, 0, 2 * st + 1]], axis=1)
        pv = _dot(jnp.concatenate([vt, jnp.ones((16, STG), BF16)], axis=0), p_ref[...])
        return pv[:DH, :BLK], pv[DH:2 * DH, BLK:], pv[2 * DH:2 * DH + 1, :]

    def fold(pv, m_pv, i):
        num0, num1, sums = pv
        scale = jnp.exp2(r_ref[i] - m_pv)
        r_ref[i] = m_pv
        l_ref[i] = scale * l_ref[i] + sums
        acc_ref[i, :DH, :] = scale[:, :BLK] * acc_ref[i, :DH, :] + num0
        acc_ref[i, DH:, :] = scale[:, BLK:] * acc_ref[i, DH:, :] + num1

    def probs_past(i, st, s_ref, p_ref, tok):
        b0 = bias_ref[i, pl.ds(2 * st, 1), :]
        b1 = bias_ref[i, pl.ds(2 * st + 1, 1), :]
        s0 = s_ref[:BLK, :]
        s1 = s_ref[BLK:, :]
        m_new = jnp.maximum(m_ref[i], jnp.maximum(jnp.max(s0, axis=0, keepdims=True) + b0,
                                                  jnp.max(s1, axis=0, keepdims=True) + b1))
        m_ref[i] = m_new
        m_new = m_new + tok
        p_ref[:BLK, :] = jnp.exp2(s0 - (m_new - b0)).astype(BF16)
        p_ref[BLK:, :] = jnp.exp2(s1 - (m_new - b1)).astype(BF16)
        return m_new

    key_idx = lax.broadcasted_iota(jnp.int32, (BLK, 2 * BLK), 0)
    qry_idx = lax.broadcasted_iota(jnp.int32, (BLK, 2 * BLK), 1) & (BLK - 1)
    causal = key_idx <= qry_idx

    def probs_diag_even(i, st, s_ref, p_ref, tok):
        s0 = jnp.where(causal, s_ref[:BLK, :], NEG)
        m_new = jnp.maximum(m_ref[i], jnp.max(s0, axis=0, keepdims=True))
        m_ref[i] = m_new
        m_new = m_new + tok
        p_ref[:BLK, :] = jnp.exp2(s0 - m_new).astype(BF16)
        p_ref[BLK:, :] = jnp.zeros((BLK, 2 * BLK), BF16)
        return m_new

    def probs_diag_odd(i, st, s_ref, p_ref, tok):
        b0 = bias_ref[i, pl.ds(2 * st, 1), :]
        s0 = s_ref[:BLK, :]
        s1 = jnp.where(causal, s_ref[BLK:, :], NEG)
        m_new = jnp.maximum(m_ref[i], jnp.maximum(jnp.max(s0, axis=0, keepdims=True) + b0,
                                                  jnp.max(s1, axis=0, keepdims=True)))
        m_ref[i] = m_new
        m_new = m_new + tok
        p_ref[:BLK, :] = jnp.exp2(s0 - (m_new - b0)).astype(BF16)
        p_ref[BLK:, :] = jnp.exp2(s1 - m_new).astype(BF16)
        return m_new

    def run_items(count, item, probs, tok0):
        def clamped(t):
            return item(jnp.clip(t, 0, count - 1))

        scores(*clamped(0), s_bufs[0])
        scores(*clamped(1), s_bufs[1])

        def slot(t, k, carry):
            m_prev, toks = carry
            scores(*clamped(t + 2), s_bufs[(k + 2) % 4])
            i_prev, st_prev = clamped(t - 1)
            pv = values(st_prev, p_bufs[(k - 1) % 4])
            i, st = clamped(t)
            m_t = probs(i, st, s_bufs[k % 4], p_bufs[k % 4], toks[0])
            fold(pv, m_prev, i_prev)
            return m_t, (toks[1], toks[2], pv[2] * 0.0)

        unroll = 4

        def body(it, carry):
            for k in range(unroll):
                carry = slot(unroll * it + k, k, carry)
            return carry

        m_last, _ = lax.fori_loop(0, count // unroll, body, (r_ref[clamped(0)[0]], (tok0, tok0, tok0)))
        i_last, st_last = clamped(count - 1)
        pv_last = values(st_last, p_bufs[(count - 1) % 4])
        fold(pv_last, m_last, i_last)
        tok_last = pv_last[2] * 0.0
        p_bufs[3][...] = jnp.broadcast_to(tok_last.astype(BF16), p_bufs[3].shape)
        return tok_last

    p_bufs[3][...] = jnp.zeros_like(p_bufs[3])
    tok = run_items(nb // 2, lambda u: (2 * u, u), probs_diag_even, jnp.zeros((1, 2 * BLK), F32))
    tok = run_items(nb // 2, lambda u: (2 * u + 1, u), probs_diag_odd, tok)
    run_items(n_items, lambda t: (tile_tab_ref[t], stage_tab_ref[t]), probs_past, tok)

    def finish(i, c):
        l = l_ref[i]
        acc = acc_ref[i]
        out_t = jnp.concatenate([acc[:DH] / l[:, :BLK], acc[DH:] / l[:, BLK:]], axis=0)
        o_ref[pl.ds(pl.multiple_of(i * BLK, BLK), BLK), :] = out_t.T.astype(BF16)
        return c

    lax.fori_loop(0, nb, finish, 0)


def _moba(mq, mk, mvt, *, batch, seq):
    BLK = MOBA_BLOCK
    nb = seq // BLK
    assert nb % 8 == 0
    npair = MOBA_W // LANES
    n_stage = nb // 2
    items = [(i, st) for st in range(n_stage) for i in range(2 * st + 2, nb)]
    assert len(items) % 4 == 0
    tile_tab = jnp.asarray([i for i, _ in items], jnp.int32)
    stage_tab = jnp.asarray([st for _, st in items], jnp.int32)
    qkspec = pl.BlockSpec((seq, LANES), lambda b, hp, *_: (b, hp))
    vtspec = pl.BlockSpec((1, 1, nb, LANES, BLK), lambda b, hp, *_: (b, hp, 0, 0, 0))
    stat = pltpu.VMEM((nb, 1, 2 * BLK), F32)
    sbuf = pltpu.VMEM((2 * BLK, 2 * BLK), F32)
    pbuf = pltpu.VMEM((2 * BLK, 2 * BLK), BF16)
    return pl.pallas_call(
        functools.partial(_moba_kernel, n_items=len(items)),
        grid_spec=pltpu.PrefetchScalarGridSpec(
            num_scalar_prefetch=2,
            grid=(batch, npair),
            in_specs=[qkspec, qkspec, vtspec],
            out_specs=qkspec,
            scratch_shapes=[pltpu.VMEM((nb, LANES), F32), pltpu.VMEM((nb, nb, 2 * BLK), F32),
                            pltpu.VMEM((nb, 2 * BLK, LANES), BF16),
                            stat, stat, stat, pltpu.VMEM((nb, LANES, BLK), F32),
                            sbuf, sbuf, sbuf, sbuf, pbuf, pbuf, pbuf, pbuf]),
        out_shape=jax.ShapeDtypeStruct((batch * seq, MOBA_W), BF16),
        compiler_params=pltpu.CompilerParams(dimension_semantics=("arbitrary",) * 2,
                                             vmem_limit_bytes=VMEM_LIMIT),
        name="moba",
    )(tile_tab, stage_tab, mq, mk, mvt)


def _merge_kernel(x_ref, ret_ref, moba_ref, ga_ref, gb_ref, wr_ref, wm_ref, wo_ref, o_ref):
    a = _dot(ret_ref[...], wr_ref[...])
    b = _dot(moba_ref[...], wm_ref[...])
    mix = ga_ref[...].astype(F32) * a + gb_ref[...].astype(F32) * b
    o_ref[...] = x_ref[...] + _dot(mix.astype(BF16), wo_ref[...])


def _merge(x2, ret, moba, sga, sgb, w_ret_out, w_moba_out, w_o, *, tm):
    T = x2.shape[0]
    row = lambda w: pl.BlockSpec((tm, w), lambda i: (i, 0))
    return pl.pallas_call(
        _merge_kernel,
        grid=(T // tm,),
        in_specs=[row(D_MODEL), row(RET_V), row(MOBA_W), row(D_MODEL), row(D_MODEL),
                  _const_spec((RET_V, D_MODEL)), _const_spec((MOBA_W, D_MODEL)), _const_spec((D_MODEL, D_MODEL))],
        out_specs=row(D_MODEL),
        out_shape=jax.ShapeDtypeStruct((T, D_MODEL), F32),
        compiler_params=pltpu.CompilerParams(dimension_semantics=("arbitrary",), vmem_limit_bytes=VMEM_LIMIT),
        name="merge",
    )(x2, ret, moba, sga, sgb, w_ret_out.astype(BF16), w_moba_out.astype(BF16), w_o.astype(BF16))


def _ffn_kernel(x_ref, n2_ref, wg_ref, wu_ref, wd_ref, o_ref):
    x = x_ref[...]
    ms = jnp.mean(x * x, axis=-1, keepdims=True)
    h = (x * lax.rsqrt(ms + RMS_EPS) * n2_ref[...]).astype(BF16)
    g = _dot(h, wg_ref[...])
    u = _dot(h, wu_ref[...])
    act = (g * jax.nn.sigmoid(g) * u).astype(BF16)
    o_ref[...] = x + _dot(act, wd_ref[...])


def _ffn(x2, norm2_w, w_gate, w_up, w_down, *, tm):
    T = x2.shape[0]
    row = pl.BlockSpec((tm, D_MODEL), lambda i: (i, 0))
    return pl.pallas_call(
        _ffn_kernel,
        grid=(T // tm,),
        in_specs=[row, _const_spec((1, D_MODEL)), _const_spec((D_MODEL, FFN_HIDDEN)),
                  _const_spec((D_MODEL, FFN_HIDDEN)), _const_spec((FFN_HIDDEN, D_MODEL))],
        out_specs=row,
        out_shape=jax.ShapeDtypeStruct((T, D_MODEL), F32),
        compiler_params=pltpu.CompilerParams(dimension_semantics=("arbitrary",), vmem_limit_bytes=VMEM_LIMIT),
        name="ffn",
    )(x2, norm2_w.astype(F32)[None, :], w_gate.astype(BF16), w_up.astype(BF16), w_down.astype(BF16))


def kernel(x, norm1_w, w_in, q_norm_w, k_norm_w, w_ret_out, w_moba_out, w_o, norm2_w, w_ffn_gate, w_ffn_up, w_ffn_down):
    B, S, D = x.shape
    assert D == D_MODEL and S % MOBA_BLOCK == 0 and S % 512 == 0
    depth = norm1_w.shape[0]
    x2 = x.reshape(B * S, D)
    for l in range(depth):
        rq, rk, rv, rgs, mq, mk, mvt, sga, sgb = _inproj(
            x2, norm1_w[l], w_in[l], q_norm_w[l], k_norm_w[l], seq=S, tm=512)
        ret = _retention(rq, rk, rv, rgs, batch=B, seq=S, rt=512)
        moba = _moba(mq, mk, mvt, batch=B, seq=S)
        x2 = _merge(x2, ret, moba, sga, sgb, w_ret_out[l], w_moba_out[l], w_o[l], tm=512)
        x2 = _ffn(x2, norm2_w[l], w_ffn_gate[l], w_ffn_up[l], w_ffn_down[l], tm=512)
    return x2.reshape(B, S, D)
```

```python
import functools
import math

import numpy as np
import jax
import jax.numpy as jnp
from jax import lax
from jax.experimental import pallas as pl
from jax.experimental.pallas import tpu as pltpu

D_MODEL = 1024
RET_HEADS = 4
RET_DK = 128
RET_DV = 256
ROPE_BASE = 10000.0
MOBA_HEADS = 8
MOBA_DH = 64
MOBA_BLOCK = 256
MOBA_TOPK = 3
FFN_HIDDEN = 2816
RET_QK = RET_HEADS * RET_DK
RET_V = RET_HEADS * RET_DV
MOBA_W = MOBA_HEADS * MOBA_DH
IN_SPLITS = (RET_QK, RET_QK, RET_V, RET_V, MOBA_W, MOBA_W, MOBA_W, D_MODEL, D_MODEL)
IN_COLS = sum(IN_SPLITS)
RMS_EPS = 1e-6
GN_EPS = 1e-5
NEG = -1e30

LANES = 128
VMEM_LIMIT = 52 * 1024 * 1024
RET_CHUNK = 256
MOBA_STAGE = 4

F32 = jnp.float32
BF16 = jnp.bfloat16


def _dot(a, b):
    return jnp.dot(a, b, preferred_element_type=F32)


def _dot_nt(a, b):
    return lax.dot_general(a, b, (((1,), (1,)), ((), ())), preferred_element_type=F32)


def _dot_tn(a, b):
    return lax.dot_general(a, b, (((0,), (0,)), ((), ())), preferred_element_type=F32)


def _const_spec(shape):
    nd = len(shape)
    return pl.BlockSpec(shape, lambda *_: (0,) * nd, pipeline_mode=pl.Buffered(1))


def _inproj_kernel(x_ref, n1_ref, w_ref, wvt_ref, qn_ref, kn_ref, inv_ref,
                   rq_ref, rk_ref, rv_ref, rg_ref, mq_ref, mk_ref, mvt_ref, ga_ref, gb_ref,
                   *, tm, seq):
    i = pl.program_id(0)
    x = x_ref[...]
    ms = jnp.mean(x * x, axis=-1, keepdims=True)
    h = (x * lax.rsqrt(ms + RMS_EPS) * n1_ref[...]).astype(BF16)

    offs = np.cumsum((0,) + IN_SPLITS)

    def proj(n):
        return _dot(h, w_ref[:, int(offs[n]):int(offs[n + 1])])

    pos0 = (i * tm) % seq
    pos = (pos0 + lax.broadcasted_iota(jnp.int32, (tm, LANES), 0)).astype(F32)
    lane = lax.broadcasted_iota(jnp.int32, (tm, LANES), 1)
    ang = pos * inv_ref[...]
    cos = jnp.cos(ang)
    sin = jnp.where(lane < RET_DK // 2, -jnp.sin(ang), jnp.sin(ang))

    def rotary_store(p, out_ref, scale):
        for hd in range(RET_HEADS):
            xh = p[:, hd * RET_DK:(hd + 1) * RET_DK]
            y = xh * cos + pltpu.roll(xh, RET_DK // 2, 1) * sin
            if scale is not None:
                y = y * scale
            out_ref[:, hd * RET_DK:(hd + 1) * RET_DK] = y.astype(BF16)

    rotary_store(proj(0), rq_ref, None)
    rotary_store(proj(1), rk_ref, RET_DK ** -0.5)
    rv_ref[...] = proj(2).astype(BF16)
    rg = proj(3)
    rg_ref[...] = (rg * jax.nn.sigmoid(rg)).astype(BF16)

    low_half = lane < MOBA_DH

    def head_rms_store(p, w_row_ref, out_ref, scale):
        for g in range(MOBA_W // LANES):
            cols = slice(g * LANES, (g + 1) * LANES)
            ph = p[:, cols]
            sq = ph * ph
            lo = jnp.sum(jnp.where(low_half, sq, 0.0), axis=-1, keepdims=True)
            hi = jnp.sum(jnp.where(low_half, 0.0, sq), axis=-1, keepdims=True)
            msq = jnp.where(low_half, lo, hi) * (1.0 / MOBA_DH)
            y = ph * lax.rsqrt(msq + RMS_EPS) * w_row_ref[:, cols]
            out_ref[:, cols] = (y if scale is None else y * scale).astype(BF16)

    head_rms_store(proj(4), qn_ref, mq_ref, MOBA_DH ** -0.5 * math.log2(math.e))
    head_rms_store(proj(5), kn_ref, mk_ref, None)
    mvt = _dot_nt(wvt_ref[...], h).astype(BF16)
    for pair in range(MOBA_W // LANES):
        for c in range(tm // MOBA_BLOCK):
            mvt_ref[0, pair, c] = mvt[pair * LANES:(pair + 1) * LANES, c * MOBA_BLOCK:(c + 1) * MOBA_BLOCK]
    ga_ref[...] = jax.nn.sigmoid(proj(7)).astype(BF16)
    gb_ref[...] = jax.nn.sigmoid(proj(8)).astype(BF16)


def _inproj(x2, norm1_w, w_in, q_norm_w, k_norm_w, *, seq, tm):
    T = x2.shape[0]
    half = RET_DK // 2
    inv = ROPE_BASE ** (-jnp.arange(half, dtype=F32) / half)
    inv2 = jnp.concatenate([inv, inv])[None, :]
    qn = jnp.tile(q_norm_w.astype(F32), MOBA_HEADS)[None, :]
    kn = jnp.tile(k_norm_w.astype(F32), MOBA_HEADS)[None, :]
    offs = np.cumsum((0,) + IN_SPLITS)
    w_mv_t = w_in[:, int(offs[6]):int(offs[7])].T.astype(BF16)
    nt = seq // tm
    npair = MOBA_W // LANES
    nb = seq // MOBA_BLOCK
    row = lambda w: pl.BlockSpec((tm, w), lambda i: (i, 0))
    row_out = lambda w: (row(w), jax.ShapeDtypeStruct((T, w), BF16))
    mvt_out = (pl.BlockSpec((1, npair, tm // MOBA_BLOCK, LANES, MOBA_BLOCK), lambda i: (i // nt, 0, i % nt, 0, 0)),
               jax.ShapeDtypeStruct((T // seq, npair, nb, LANES, MOBA_BLOCK), BF16))
    outs = [row_out(RET_QK), row_out(RET_QK), row_out(RET_V), row_out(RET_V), row_out(MOBA_W), row_out(MOBA_W),
            mvt_out, row_out(D_MODEL), row_out(D_MODEL)]
    return pl.pallas_call(
        functools.partial(_inproj_kernel, tm=tm, seq=seq),
        grid=(T // tm,),
        in_specs=[row(D_MODEL), _const_spec((1, D_MODEL)), _const_spec((D_MODEL, IN_COLS)),
                  _const_spec((MOBA_W, D_MODEL)),
                  _const_spec((1, MOBA_W)), _const_spec((1, MOBA_W)), _const_spec((1, LANES))],
        out_specs=[o[0] for o in outs],
        out_shape=[o[1] for o in outs],
        compiler_params=pltpu.CompilerParams(dimension_semantics=("arbitrary",), vmem_limit_bytes=VMEM_LIMIT),
        name="inproj",
    )(x2, norm1_w.astype(F32)[None, :], w_in.astype(BF16), w_mv_t, qn, kn, inv2)


def _retention_kernel(q_ref, k_ref, v_ref, g_ref, dmat_ref, xi_ref, zeta_ref, o_ref, state_ref,
                      *, rt, chunk, g_chunk):
    @pl.when(pl.program_id(1) == 0)
    def _():
        state_ref[...] = jnp.zeros_like(state_ref)

    for c in range(rt // chunk):
        rows = slice(c * chunk, (c + 1) * chunk)
        for hd in range(RET_HEADS):
            kcols = slice(hd * RET_DK, (hd + 1) * RET_DK)
            vcols = slice(hd * RET_DV, (hd + 1) * RET_DV)
            q = q_ref[rows, kcols]
            k = k_ref[rows, kcols]
            v = v_ref[rows, vcols]
            state = state_ref[hd]
            scores = _dot_nt(q, k) * dmat_ref[hd]
            q_dec = (q.astype(F32) * xi_ref[hd]).astype(BF16)
            o = _dot(scores.astype(BF16), v) + _dot(q_dec, state.astype(BF16))
            k_dec = (k.astype(F32) * zeta_ref[hd]).astype(BF16)
            state_ref[hd] = g_chunk[hd] * state + _dot_tn(k_dec, v)
            mu = jnp.mean(o, axis=-1, keepdims=True)
            d = o - mu
            var = jnp.mean(d * d, axis=-1, keepdims=True)
            y = d * lax.rsqrt(var + GN_EPS)
            o_ref[rows, vcols] = (g_ref[rows, vcols].astype(F32) * y).astype(BF16)


def _retention(rq, rk, rv, rgs, *, batch, seq, rt):
    C = RET_CHUNK
    lg = np.log1p(-np.exp2(-5.0 - np.arange(RET_HEADS, dtype=np.float64)))
    idx = np.arange(C, dtype=np.float64)
    diff = idx[:, None] - idx[None, :]
    dmat = np.where(diff >= 0, np.exp(np.maximum(diff, 0.0)[None] * lg[:, None, None]), 0.0)
    xi = np.broadcast_to(np.exp((idx + 1.0)[None, :] * lg[:, None])[:, :, None], (RET_HEADS, C, LANES))
    zeta = np.broadcast_to(np.exp((C - 1.0 - idx)[None, :] * lg[:, None])[:, :, None], (RET_HEADS, C, LANES))
    g_chunk = tuple(float(v) for v in np.exp(C * lg))
    nt = seq // rt
    row = lambda w: pl.BlockSpec((rt, w), lambda b, j: (b * nt + j, 0))
    return pl.pallas_call(
        functools.partial(_retention_kernel, rt=rt, chunk=C, g_chunk=g_chunk),
        grid=(batch, nt),
        in_specs=[row(RET_QK), row(RET_QK), row(RET_V), row(RET_V),
                  _const_spec((RET_HEADS, C, C)), _const_spec((RET_HEADS, C, LANES)),
                  _const_spec((RET_HEADS, C, LANES))],
        out_specs=row(RET_V),
        out_shape=jax.ShapeDtypeStruct((batch * seq, RET_V), BF16),
        scratch_shapes=[pltpu.VMEM((RET_HEADS, RET_DK, RET_DV), F32)],
        compiler_params=pltpu.CompilerParams(dimension_semantics=("arbitrary", "arbitrary"),
                                             vmem_limit_bytes=VMEM_LIMIT),
        name="retention",
    )(rq, rk, rv, rgs, jnp.asarray(dmat, F32), jnp.asarray(xi, F32), jnp.asarray(zeta, F32))


def _moba_kernel(tile_tab_ref, stage_tab_ref, q_ref, k_ref, vt_ref, o_ref,
                 kbar_ref, bias_ref, qcat_ref, m_ref, r_ref, l_ref, acc_ref,
                 s0_ref, s1_ref, s2_ref, s3_ref, p0_ref, p1_ref, p2_ref, p3_ref, *, n_items):
    BLK = MOBA_BLOCK
    DH = MOBA_DH
    G = MOBA_STAGE
    STG = G * BLK
    nb = vt_ref.shape[2]
    nbp = kbar_ref.shape[0]
    s_bufs = (s0_ref, s1_ref, s2_ref, s3_ref)
    p_bufs = (p0_ref, p1_ref, p2_ref, p3_ref)
    lane = lax.broadcasted_iota(jnp.int32, (1, LANES), 1)

    def q_cat_body(i, c):
        q_pair = q_ref[pl.ds(pl.multiple_of(i * BLK, BLK), BLK), :]
        zero = jnp.zeros_like(q_pair)
        qcat_ref[i, :BLK, :] = jnp.where(lane < DH, q_pair, zero)
        qcat_ref[i, BLK:, :] = jnp.where(lane >= DH, q_pair, zero)
        return c

    lax.fori_loop(0, nb, q_cat_body, 0)

    def q_cat_of(i):
        return qcat_ref[i]

    kbar_ref[...] = jnp.zeros_like(kbar_ref)

    def kbar_body(n, c):
        kb = k_ref[pl.ds(pl.multiple_of(n * BLK, BLK), BLK), :]
        kbar_ref[pl.ds(n, 1), :] = jnp.mean(kb.astype(F32), axis=0, keepdims=True)
        return c

    lax.fori_loop(0, nb, kbar_body, 0)

    kbar = kbar_ref[...]
    kbar_hi = kbar.astype(BF16)
    kbar_lo = (kbar - kbar_hi.astype(F32)).astype(BF16)
    blk_idx = lax.broadcasted_iota(jnp.int32, (nbp, 2 * BLK), 0)

    def select_blocks(i):
        q_cat = q_cat_of(i)
        gate = _dot_nt(kbar_hi, q_cat) + _dot_nt(kbar_lo, q_cat)
        past = blk_idx < i
        gate = jnp.where(past, gate, NEG)
        sel = jnp.zeros((nbp, 2 * BLK), jnp.bool_)
        for _ in range(MOBA_TOPK):
            best = jnp.max(gate, axis=0, keepdims=True)
            first = jnp.min(jnp.where(gate == best, blk_idx, nbp), axis=0, keepdims=True)
            pick = blk_idx == first
            sel = sel | (pick & past)
            gate = jnp.where(pick, -3e38, gate)
        bias_ref[i] = jnp.where(sel, 0.0, NEG)

    def select_body(u, c):
        select_blocks(2 * u)
        select_blocks(2 * u + 1)
        return c

    lax.fori_loop(0, nb // 2, select_body, 0)

    m_ref[...] = jnp.full(m_ref.shape, NEG, F32)
    r_ref[...] = jnp.full(r_ref.shape, NEG, F32)
    l_ref[...] = jnp.zeros_like(l_ref)
    acc_ref[...] = jnp.zeros_like(acc_ref)

    def scores(i, st, dst_ref):
        ks = k_ref[pl.ds(pl.multiple_of(st * STG, STG), STG), :]
        dst_ref[...] = _dot_nt(ks, q_cat_of(i))

    def values(st, p_ref):
        vt = jnp.concatenate([vt_ref[0, 0, G * st + g] for g in range(G)], axis=1)
        pv = _dot(jnp.concatenate([vt, jnp.ones((16, STG), BF16)], axis=0), p_ref[...])
        return pv[:DH, :BLK], pv[DH:2 * DH, BLK:], pv[2 * DH:2 * DH + 1, :]

    def fold(pv, m_pv, i):
        num0, num1, sums = pv
        scale = jnp.exp2(r_ref[i] - m_pv)
        r_ref[i] = m_pv
        l_ref[i] = scale * l_ref[i] + sums
        acc_ref[i, :DH, :] = scale[:, :BLK] * acc_ref[i, :DH, :] + num0
        acc_ref[i, DH:, :] = scale[:, BLK:] * acc_ref[i, DH:, :] + num1

    key_idx = lax.broadcasted_iota(jnp.int32, (BLK, 2 * BLK), 0)
    qry_idx = lax.broadcasted_iota(jnp.int32, (BLK, 2 * BLK), 1) & (BLK - 1)
    causal = key_idx <= qry_idx

    def make_probs(own):
        def probs(i, st, s_ref, p_ref, tok):
            n_live = G if own is None else own + 1
            tiles, shifts = [], []
            m_new = m_ref[i]
            for g in range(n_live):
                sg = s_ref[g * BLK:(g + 1) * BLK, :]
                if g == own:
                    sg = jnp.where(causal, sg, NEG)
                    bg = None
                    m_new = jnp.maximum(m_new, jnp.max(sg, axis=0, keepdims=True))
                else:
                    bg = bias_ref[i, pl.ds(G * st + g, 1), :]
                    m_new = jnp.maximum(m_new, jnp.max(sg, axis=0, keepdims=True) + bg)
                tiles.append(sg)
                shifts.append(bg)
            m_ref[i] = m_new
            m_new = m_new + tok
            for g in range(G):
                rows = slice(g * BLK, (g + 1) * BLK)
                if g >= n_live:
                    p_ref[rows, :] = jnp.zeros((BLK, 2 * BLK), BF16)
                else:
                    shift = m_new if shifts[g] is None else m_new - shifts[g]
                    p_ref[rows, :] = jnp.exp2(tiles[g] - shift).astype(BF16)
            return m_new
        return probs

    def run_items(count, item, probs, tok0):
        def clamped(t):
            return item(jnp.clip(t, 0, count - 1))

        scores(*clamped(0), s_bufs[0])
        scores(*clamped(1), s_bufs[1])

        def slot(t, k, carry):
            m_prev, toks = carry
            scores(*clamped(t + 2), s_bufs[(k + 2) % 4])
            i_prev, st_prev = clamped(t - 1)
            pv = values(st_prev, p_bufs[(k - 1) % 4])
            i, st = clamped(t)
            m_t = probs(i, st, s_bufs[k % 4], p_bufs[k % 4], toks[0])
            fold(pv, m_prev, i_prev)
            return m_t, (toks[1], toks[2], pv[2] * 0.0)

        unroll = 4

        def body(it, carry):
            for k in range(unroll):
                carry = slot(unroll * it + k, k, carry)
            return carry

        m_last, _ = lax.fori_loop(0, count // unroll, body, (r_ref[clamped(0)[0]], (tok0, tok0, tok0)))
        i_last, st_last = clamped(count - 1)
        pv_last = values(st_last, p_bufs[(count - 1) % 4])
        fold(pv_last, m_last, i_last)
        tok_last = pv_last[2] * 0.0
        p_bufs[3][...] = jnp.broadcast_to(tok_last.astype(BF16), p_bufs[3].shape)
        return tok_last

    p_bufs[3][...] = jnp.zeros_like(p_bufs[3])
    tok = jnp.zeros((1, 2 * BLK), F32)
    for own in range(G):
        tok = run_items(nb // G, lambda u, own=own: (G * u + own, u), make_probs(own), tok)
    run_items(n_items, lambda t: (tile_tab_ref[t], stage_tab_ref[t]), make_probs(None), tok)

    def finish(i, c):
        l = l_ref[i]
        acc = acc_ref[i]
        out_t = jnp.concatenate([acc[:DH] / l[:, :BLK], acc[DH:] / l[:, BLK:]], axis=0)
        o_ref[pl.ds(pl.multiple_of(i * BLK, BLK), BLK), :] = out_t.T.astype(BF16)
        return c

    lax.fori_loop(0, nb, finish, 0)


def _moba(mq, mk, mvt, *, batch, seq):
    BLK = MOBA_BLOCK
    nb = seq // BLK
    G = MOBA_STAGE
    assert nb % (4 * G) == 0
    npair = MOBA_W // LANES
    items = [(i, st) for st in range(nb // G) for i in range(G * st + G, nb)]
    assert len(items) % 4 == 0
    tile_tab = jnp.asarray([i for i, _ in items], jnp.int32)
    stage_tab = jnp.asarray([st for _, st in items], jnp.int32)
    qkspec = pl.BlockSpec((seq, LANES), lambda b, hp, *_: (b, hp))
    vtspec = pl.BlockSpec((1, 1, nb, LANES, BLK), lambda b, hp, *_: (b, hp, 0, 0, 0))
    stat = pltpu.VMEM((nb, 1, 2 * BLK), F32)
    sbuf = pltpu.VMEM((G * BLK, 2 * BLK), F32)
    pbuf = pltpu.VMEM((G * BLK, 2 * BLK), BF16)
    return pl.pallas_call(
        functools.partial(_moba_kernel, n_items=len(items)),
        grid_spec=pltpu.PrefetchScalarGridSpec(
            num_scalar_prefetch=2,
            grid=(batch, npair),
            in_specs=[qkspec, qkspec, vtspec],
            out_specs=qkspec,
            scratch_shapes=[pltpu.VMEM((nb, LANES), F32), pltpu.VMEM((nb, nb, 2 * BLK), F32),
                            pltpu.VMEM((nb, 2 * BLK, LANES), BF16),
                            stat, stat, stat, pltpu.VMEM((nb, LANES, BLK), F32),
                            sbuf, sbuf, sbuf, sbuf, pbuf, pbuf, pbuf, pbuf]),
        out_shape=jax.ShapeDtypeStruct((batch * seq, MOBA_W), BF16),
        compiler_params=pltpu.CompilerParams(dimension_semantics=("arbitrary",) * 2,
                                             vmem_limit_bytes=VMEM_LIMIT),
        name="moba",
    )(tile_tab, stage_tab, mq, mk, mvt)


def _merge_kernel(x_ref, ret_ref, moba_ref, ga_ref, gb_ref, wr_ref, wm_ref, wo_ref, o_ref):
    a = _dot(ret_ref[...], wr_ref[...])
    b = _dot(moba_ref[...], wm_ref[...])
    mix = ga_ref[...].astype(F32) * a + gb_ref[...].astype(F32) * b
    o_ref[...] = x_ref[...] + _dot(mix.astype(BF16), wo_ref[...])


def _merge(x2, ret, moba, sga, sgb, w_ret_out, w_moba_out, w_o, *, tm):
    T = x2.shape[0]
    row = lambda w: pl.BlockSpec((tm, w), lambda i: (i, 0))
    return pl.pallas_call(
        _merge_kernel,
        grid=(T // tm,),
        in_specs=[row(D_MODEL), row(RET_V), row(MOBA_W), row(D_MODEL), row(D_MODEL),
                  _const_spec((RET_V, D_MODEL)), _const_spec((MOBA_W, D_MODEL)), _const_spec((D_MODEL, D_MODEL))],
        out_specs=row(D_MODEL),
        out_shape=jax.ShapeDtypeStruct((T, D_MODEL), F32),
        compiler_params=pltpu.CompilerParams(dimension_semantics=("arbitrary",), vmem_limit_bytes=VMEM_LIMIT),
        name="merge",
    )(x2, ret, moba, sga, sgb, w_ret_out.astype(BF16), w_moba_out.astype(BF16), w_o.astype(BF16))


def _ffn_kernel(x_ref, n2_ref, wg_ref, wu_ref, wd_ref, o_ref):
    x = x_ref[...]
    ms = jnp.mean(x * x, axis=-1, keepdims=True)
    h = (x * lax.rsqrt(ms + RMS_EPS) * n2_ref[...]).astype(BF16)
    g = _dot(h, wg_ref[...])
    u = _dot(h, wu_ref[...])
    act = (g * jax.nn.sigmoid(g) * u).astype(BF16)
    o_ref[...] = x + _dot(act, wd_ref[...])


def _ffn(x2, norm2_w, w_gate, w_up, w_down, *, tm):
    T = x2.shape[0]
    row = pl.BlockSpec((tm, D_MODEL), lambda i: (i, 0))
    return pl.pallas_call(
        _ffn_kernel,
        grid=(T // tm,),
        in_specs=[row, _const_spec((1, D_MODEL)), _const_spec((D_MODEL, FFN_HIDDEN)),
                  _const_spec((D_MODEL, FFN_HIDDEN)), _const_spec((FFN_HIDDEN, D_MODEL))],
        out_specs=row,
        out_shape=jax.ShapeDtypeStruct((T, D_MODEL), F32),
        compiler_params=pltpu.CompilerParams(dimension_semantics=("arbitrary",), vmem_limit_bytes=VMEM_LIMIT),
        name="ffn",
    )(x2, norm2_w.astype(F32)[None, :], w_gate.astype(BF16), w_up.astype(BF16), w_down.astype(BF16))


def kernel(x, norm1_w, w_in, q_norm_w, k_norm_w, w_ret_out, w_moba_out, w_o, norm2_w, w_ffn_gate, w_ffn_up, w_ffn_down):
    B, S, D = x.shape
    assert D == D_MODEL and S % MOBA_BLOCK == 0 and S % 512 == 0
    depth = norm1_w.shape[0]
    x2 = x.reshape(B * S, D)
    for l in range(depth):
        rq, rk, rv, rgs, mq, mk, mvt, sga, sgb = _inproj(
            x2, norm1_w[l], w_in[l], q_norm_w[l], k_norm_w[l], seq=S, tm=512)
        ret = _retention(rq, rk, rv, rgs, batch=B, seq=S, rt=512)
        moba = _moba(mq, mk, mvt, batch=B, seq=S)
        x2 = _merge(x2, ret, moba, sga, sgb, w_ret_out[l], w_moba_out[l], w_o[l], tm=512)
        x2 = _ffn(x2, norm2_w[l], w_ffn_gate[l], w_ffn_up[l], w_ffn_down[l], tm=512)
    return x2.reshape(B, S, D)
```

```python
import functools
import math

import numpy as np
import jax
import jax.numpy as jnp
from jax import lax
from jax.experimental import pallas as pl
from jax.experimental.pallas import tpu as pltpu

D_MODEL = 1024
RET_HEADS = 4
RET_DK = 128
RET_DV = 256
ROPE_BASE = 10000.0
MOBA_HEADS = 8
MOBA_DH = 64
MOBA_BLOCK = 256
MOBA_TOPK = 3
FFN_HIDDEN = 2816
RET_QK = RET_HEADS * RET_DK
RET_V = RET_HEADS * RET_DV
MOBA_W = MOBA_HEADS * MOBA_DH
IN_SPLITS = (RET_QK, RET_QK, RET_V, RET_V, MOBA_W, MOBA_W, MOBA_W, D_MODEL, D_MODEL)
IN_COLS = sum(IN_SPLITS)
RMS_EPS = 1e-6
GN_EPS = 1e-5
NEG = -1e30

LANES = 128
VMEM_LIMIT = 52 * 1024 * 1024
RET_CHUNK = 256
MOBA_STAGE = 4
ROW_TILE = 512
MERGE_ROW_TILE = 1024

F32 = jnp.float32
BF16 = jnp.bfloat16


def _dot(a, b):
    return jnp.dot(a, b, preferred_element_type=F32)


def _dot_nt(a, b):
    return lax.dot_general(a, b, (((1,), (1,)), ((), ())), preferred_element_type=F32)


def _dot_tn(a, b):
    return lax.dot_general(a, b, (((0,), (0,)), ((), ())), preferred_element_type=F32)


def _const_spec(shape):
    nd = len(shape)
    return pl.BlockSpec(shape, lambda *_: (0,) * nd, pipeline_mode=pl.Buffered(1))


def _inproj_kernel(x_ref, xn_ref, n1_ref, w_ref, wvt_ref, qn_ref, kn_ref, cos_ref, sin_ref,
                   rq_ref, rk_ref, rv_ref, rg_ref, mq_ref, mk_ref, mvt_ref, ga_ref, gb_ref, h_ref,
                   *, tm):
    i = pl.program_id(0)

    def normed(x):
        ms = jnp.mean(x * x, axis=-1, keepdims=True)
        return (x * lax.rsqrt(ms + RMS_EPS) * n1_ref[...]).astype(BF16)

    @pl.when(i == 0)
    def _():
        h_ref[0] = normed(x_ref[...])

    h = h_ref[i % 2]

    offs = np.cumsum((0,) + IN_SPLITS)

    def proj(n):
        return _dot(h, w_ref[:, int(offs[n]):int(offs[n + 1])])

    cos = cos_ref[...]
    sin = sin_ref[...]
    lane = lax.broadcasted_iota(jnp.int32, (tm, LANES), 1)

    def rotary_store(p, out_ref, scale):
        for hd in range(RET_HEADS):
            xh = p[:, hd * RET_DK:(hd + 1) * RET_DK]
            y = xh * cos + pltpu.roll(xh, RET_DK // 2, 1) * sin
            if scale is not None:
                y = y * scale
            out_ref[:, hd * RET_DK:(hd + 1) * RET_DK] = y.astype(BF16)

    rotary_store(proj(0), rq_ref, None)
    rotary_store(proj(1), rk_ref, RET_DK ** -0.5)
    rv_ref[...] = proj(2).astype(BF16)
    rg = proj(3)
    rg_ref[...] = (rg * jax.nn.sigmoid(rg)).astype(BF16)

    low_half = lane < MOBA_DH

    def head_rms_store(p, w_row_ref, out_ref, scale):
        for g in range(MOBA_W // LANES):
            cols = slice(g * LANES, (g + 1) * LANES)
            ph = p[:, cols]
            sq = ph * ph
            lo = jnp.sum(jnp.where(low_half, sq, 0.0), axis=-1, keepdims=True)
            hi = jnp.sum(jnp.where(low_half, 0.0, sq), axis=-1, keepdims=True)
            msq = jnp.where(low_half, lo, hi) * (1.0 / MOBA_DH)
            y = ph * lax.rsqrt(msq + RMS_EPS) * w_row_ref[:, cols]
            out_ref[:, cols] = (y if scale is None else y * scale).astype(BF16)

    head_rms_store(proj(4), qn_ref, mq_ref, MOBA_DH ** -0.5 * math.log2(math.e))
    head_rms_store(proj(5), kn_ref, mk_ref, None)
    mvt = _dot_nt(wvt_ref[...], h).astype(BF16)
    for pair in range(MOBA_W // LANES):
        for c in range(tm // MOBA_BLOCK):
            mvt_ref[0, pair, c] = mvt[pair * LANES:(pair + 1) * LANES, c * MOBA_BLOCK:(c + 1) * MOBA_BLOCK]
    ga_ref[...] = jax.nn.sigmoid(proj(7)).astype(BF16)
    gb_ref[...] = jax.nn.sigmoid(proj(8)).astype(BF16)
    h_ref[(i + 1) % 2] = normed(xn_ref[...])


def _inproj(x2, norm1_w, w_in, q_norm_w, k_norm_w, *, seq, tm):
    T = x2.shape[0]
    half = RET_DK // 2
    inv = ROPE_BASE ** (-jnp.arange(half, dtype=F32) / half)
    ang = jnp.arange(seq).astype(F32)[:, None] * inv[None, :]
    cos2 = jnp.concatenate([jnp.cos(ang), jnp.cos(ang)], axis=1)
    sin2 = jnp.concatenate([-jnp.sin(ang), jnp.sin(ang)], axis=1)
    qn = jnp.tile(q_norm_w.astype(F32), MOBA_HEADS)[None, :]
    kn = jnp.tile(k_norm_w.astype(F32), MOBA_HEADS)[None, :]
    offs = np.cumsum((0,) + IN_SPLITS)
    w_mv_t = w_in[:, int(offs[6]):int(offs[7])].T.astype(BF16)
    n_tiles = T // tm
    nt = seq // tm
    npair = MOBA_W // LANES
    nb = seq // MOBA_BLOCK
    row = lambda w: pl.BlockSpec((tm, w), lambda i: (i, 0))
    next_row = pl.BlockSpec((tm, D_MODEL), lambda i: (jnp.minimum(i + 1, n_tiles - 1), 0))
    pos_row = pl.BlockSpec((tm, LANES), lambda i: (i % nt, 0))
    row_out = lambda w: (row(w), jax.ShapeDtypeStruct((T, w), BF16))
    mvt_out = (pl.BlockSpec((1, npair, tm // MOBA_BLOCK, LANES, MOBA_BLOCK), lambda i: (i // nt, 0, i % nt, 0, 0)),
               jax.ShapeDtypeStruct((T // seq, npair, nb, LANES, MOBA_BLOCK), BF16))
    outs = [row_out(RET_QK), row_out(RET_QK), row_out(RET_V), row_out(RET_V), row_out(MOBA_W), row_out(MOBA_W),
            mvt_out, row_out(D_MODEL), row_out(D_MODEL)]
    return pl.pallas_call(
        functools.partial(_inproj_kernel, tm=tm),
        grid=(n_tiles,),
        in_specs=[row(D_MODEL), next_row, _const_spec((1, D_MODEL)), _const_spec((D_MODEL, IN_COLS)),
                  _const_spec((MOBA_W, D_MODEL)),
                  _const_spec((1, MOBA_W)), _const_spec((1, MOBA_W)), pos_row, pos_row],
        out_specs=[o[0] for o in outs],
        out_shape=[o[1] for o in outs],
        scratch_shapes=[pltpu.VMEM((2, tm, D_MODEL), BF16)],
        compiler_params=pltpu.CompilerParams(dimension_semantics=("arbitrary",), vmem_limit_bytes=VMEM_LIMIT),
        name="inproj",
    )(x2, x2, norm1_w.astype(F32)[None, :], w_in.astype(BF16), w_mv_t, qn, kn, cos2, sin2)


def _retention_kernel(q_ref, k_ref, v_ref, g_ref, dmat_ref, xi_ref, zeta_ref, o_ref, state_ref,
                      *, rt, chunk, g_chunk):
    @pl.when(pl.program_id(1) == 0)
    def _():
        state_ref[...] = jnp.zeros_like(state_ref)

    for c in range(rt // chunk):
        rows = slice(c * chunk, (c + 1) * chunk)
        for hd in range(RET_HEADS):
            kcols = slice(hd * RET_DK, (hd + 1) * RET_DK)
            vcols = slice(hd * RET_DV, (hd + 1) * RET_DV)
            q = q_ref[rows, kcols]
            k = k_ref[rows, kcols]
            v = v_ref[rows, vcols]
            state = state_ref[hd]
            scores = _dot_nt(q, k) * dmat_ref[hd]
            q_dec = (q.astype(F32) * xi_ref[hd]).astype(BF16)
            o = _dot(scores.astype(BF16), v) + _dot(q_dec, state.astype(BF16))
            k_dec = (k.astype(F32) * zeta_ref[hd]).astype(BF16)
            state_ref[hd] = g_chunk[hd] * state + _dot_tn(k_dec, v)
            mu = jnp.mean(o, axis=-1, keepdims=True)
            d = o - mu
            var = jnp.mean(d * d, axis=-1, keepdims=True)
            y = d * lax.rsqrt(var + GN_EPS)
            o_ref[rows, vcols] = (g_ref[rows, vcols].astype(F32) * y).astype(BF16)


def _retention(rq, rk, rv, rgs, *, batch, seq, rt):
    C = RET_CHUNK
    lg = np.log1p(-np.exp2(-5.0 - np.arange(RET_HEADS, dtype=np.float64)))
    idx = np.arange(C, dtype=np.float64)
    diff = idx[:, None] - idx[None, :]
    dmat = np.where(diff >= 0, np.exp(np.maximum(diff, 0.0)[None] * lg[:, None, None]), 0.0)
    xi = np.broadcast_to(np.exp((idx + 1.0)[None, :] * lg[:, None])[:, :, None], (RET_HEADS, C, LANES))
    zeta = np.broadcast_to(np.exp((C - 1.0 - idx)[None, :] * lg[:, None])[:, :, None], (RET_HEADS, C, LANES))
    g_chunk = tuple(float(v) for v in np.exp(C * lg))
    nt = seq // rt
    row = lambda w: pl.BlockSpec((rt, w), lambda b, j: (b * nt + j, 0))
    return pl.pallas_call(
        functools.partial(_retention_kernel, rt=rt, chunk=C, g_chunk=g_chunk),
        grid=(batch, nt),
        in_specs=[row(RET_QK), row(RET_QK), row(RET_V), row(RET_V),
                  _const_spec((RET_HEADS, C, C)), _const_spec((RET_HEADS, C, LANES)),
                  _const_spec((RET_HEADS, C, LANES))],
        out_specs=row(RET_V),
        out_shape=jax.ShapeDtypeStruct((batch * seq, RET_V), BF16),
        scratch_shapes=[pltpu.VMEM((RET_HEADS, RET_DK, RET_DV), F32)],
        compiler_params=pltpu.CompilerParams(dimension_semantics=("arbitrary", "arbitrary"),
                                             vmem_limit_bytes=VMEM_LIMIT),
        name="retention",
    )(rq, rk, rv, rgs, jnp.asarray(dmat, F32), jnp.asarray(xi, F32), jnp.asarray(zeta, F32))


def _moba_kernel(tile_tab_ref, stage_tab_ref, q_ref, k_ref, vt_ref, o_ref,
                 kbar_ref, bias_ref, qcat_ref, m_ref, r_ref, l_ref, acc_ref,
                 s0_ref, s1_ref, s2_ref, s3_ref, p0_ref, p1_ref, p2_ref, p3_ref, *, n_items):
    BLK = MOBA_BLOCK
    DH = MOBA_DH
    G = MOBA_STAGE
    STG = G * BLK
    nb = vt_ref.shape[2]
    nbp = kbar_ref.shape[0]
    s_bufs = (s0_ref, s1_ref, s2_ref, s3_ref)
    p_bufs = (p0_ref, p1_ref, p2_ref, p3_ref)
    lane = lax.broadcasted_iota(jnp.int32, (1, LANES), 1)

    def q_cat_body(i, c):
        q_pair = q_ref[pl.ds(pl.multiple_of(i * BLK, BLK), BLK), :]
        zero = jnp.zeros_like(q_pair)
        qcat_ref[i, :BLK, :] = jnp.where(lane < DH, q_pair, zero)
        qcat_ref[i, BLK:, :] = jnp.where(lane >= DH, q_pair, zero)
        return c

    lax.fori_loop(0, nb, q_cat_body, 0)

    def q_cat_of(i):
        return qcat_ref[i]

    kbar_ref[...] = jnp.zeros_like(kbar_ref)

    def kbar_body(n, c):
        kb = k_ref[pl.ds(pl.multiple_of(n * BLK, BLK), BLK), :]
        kbar_ref[pl.ds(n, 1), :] = jnp.mean(kb.astype(F32), axis=0, keepdims=True)
        return c

    lax.fori_loop(0, nb, kbar_body, 0)

    kbar = kbar_ref[...]
    kbar_hi = kbar.astype(BF16)
    kbar_lo = (kbar - kbar_hi.astype(F32)).astype(BF16)
    blk_idx = lax.broadcasted_iota(jnp.int32, (nbp, 2 * BLK), 0)

    def select_blocks(i):
        q_cat = q_cat_of(i)
        gate = _dot_nt(kbar_hi, q_cat) + _dot_nt(kbar_lo, q_cat)
        past = blk_idx < i
        gate = jnp.where(past, gate, NEG)
        sel = jnp.zeros((nbp, 2 * BLK), jnp.bool_)
        for _ in range(MOBA_TOPK):
            best = jnp.max(gate, axis=0, keepdims=True)
            first = jnp.min(jnp.where(gate == best, blk_idx, nbp), axis=0, keepdims=True)
            pick = blk_idx == first
            sel = sel | (pick & past)
            gate = jnp.where(pick, -3e38, gate)
        bias_ref[i] = jnp.where(sel, 0.0, NEG)

    def select_body(u, c):
        select_blocks(2 * u)
        select_blocks(2 * u + 1)
        return c

    lax.fori_loop(0, nb // 2, select_body, 0)

    m_ref[...] = jnp.full(m_ref.shape, NEG, F32)
    r_ref[...] = jnp.full(r_ref.shape, NEG, F32)
    l_ref[...] = jnp.zeros_like(l_ref)
    acc_ref[...] = jnp.zeros_like(acc_ref)

    def scores(i, st, dst_ref):
        ks = k_ref[pl.ds(pl.multiple_of(st * STG, STG), STG), :]
        dst_ref[...] = _dot_nt(ks, q_cat_of(i))

    def values(st, p_ref):
        vt = jnp.concatenate([vt_ref[0, 0, G * st + g] for g in range(G)], axis=1)
        pv = _dot(jnp.concatenate([vt, jnp.ones((16, STG), BF16)], axis=0), p_ref[...])
        return pv[:DH, :BLK], pv[DH:2 * DH, BLK:], pv[2 * DH:2 * DH + 1, :]

    def fold(pv, m_pv, i):
        num0, num1, sums = pv
        scale = jnp.exp2(r_ref[i] - m_pv)
        r_ref[i] = m_pv
        l_ref[i] = scale * l_ref[i] + sums
        acc_ref[i, :DH, :] = scale[:, :BLK] * acc_ref[i, :DH, :] + num0
        acc_ref[i, DH:, :] = scale[:, BLK:] * acc_ref[i, DH:, :] + num1

    key_idx = lax.broadcasted_iota(jnp.int32, (BLK, 2 * BLK), 0)
    qry_idx = lax.broadcasted_iota(jnp.int32, (BLK, 2 * BLK), 1) & (BLK - 1)
    causal = key_idx <= qry_idx

    def make_probs(own):
        def probs(i, st, s_ref, p_ref, tok):
            n_live = G if own is None else own + 1
            tiles, shifts = [], []
            m_new = m_ref[i]
            for g in range(n_live):
                sg = s_ref[g * BLK:(g + 1) * BLK, :]
                if g == own:
                    sg = jnp.where(causal, sg, NEG)
                    bg = None
                    m_new = jnp.maximum(m_new, jnp.max(sg, axis=0, keepdims=True))
                else:
                    bg = bias_ref[i, pl.ds(G * st + g, 1), :]
                    m_new = jnp.maximum(m_new, jnp.max(sg, axis=0, keepdims=True) + bg)
                tiles.append(sg)
                shifts.append(bg)
            m_ref[i] = m_new
            m_new = m_new + tok
            for g in range(G):
                rows = slice(g * BLK, (g + 1) * BLK)
                if g >= n_live:
                    p_ref[rows, :] = jnp.zeros((BLK, 2 * BLK), BF16)
                else:
                    shift = m_new if shifts[g] is None else m_new - shifts[g]
                    p_ref[rows, :] = jnp.exp2(tiles[g] - shift).astype(BF16)
            return m_new
        return probs

    def run_items(count, item, probs, tok0):
        def clamped(t):
            return item(jnp.clip(t, 0, count - 1))

        scores(*clamped(0), s_bufs[0])
        scores(*clamped(1), s_bufs[1])

        def slot(t, k, carry):
            m_prev, toks = carry
            scores(*clamped(t + 2), s_bufs[(k + 2) % 4])
            i_prev, st_prev = clamped(t - 1)
            pv = values(st_prev, p_bufs[(k - 1) % 4])
            i, st = clamped(t)
            m_t = probs(i, st, s_bufs[k % 4], p_bufs[k % 4], toks[0])
            fold(pv, m_prev, i_prev)
            return m_t, (toks[1], toks[2], pv[2] * 0.0)

        unroll = 4

        def body(it, carry):
            for k in range(unroll):
                carry = slot(unroll * it + k, k, carry)
            return carry

        m_last, _ = lax.fori_loop(0, count // unroll, body, (r_ref[clamped(0)[0]], (tok0, tok0, tok0)))
        i_last, st_last = clamped(count - 1)
        pv_last = values(st_last, p_bufs[(count - 1) % 4])
        fold(pv_last, m_last, i_last)
        tok_last = pv_last[2] * 0.0
        p_bufs[3][...] = jnp.broadcast_to(tok_last.astype(BF16), p_bufs[3].shape)
        return tok_last

    p_bufs[3][...] = jnp.zeros_like(p_bufs[3])
    tok = jnp.zeros((1, 2 * BLK), F32)
    for own in range(G):
        tok = run_items(nb // G, lambda u, own=own: (G * u + own, u), make_probs(own), tok)
    run_items(n_items, lambda t: (tile_tab_ref[t], stage_tab_ref[t]), make_probs(None), tok)

    def finish(i, c):
        l = l_ref[i]
        acc = acc_ref[i]
        out_t = jnp.concatenate([acc[:DH] / l[:, :BLK], acc[DH:] / l[:, BLK:]], axis=0)
        o_ref[pl.ds(pl.multiple_of(i * BLK, BLK), BLK), :] = out_t.T.astype(BF16)
        return c

    lax.fori_loop(0, nb, finish, 0)


def _moba(mq, mk, mvt, *, batch, seq):
    BLK = MOBA_BLOCK
    nb = seq // BLK
    G = MOBA_STAGE
    assert nb % (4 * G) == 0
    npair = MOBA_W // LANES
    items = [(i, st) for st in range(nb // G) for i in range(G * st + G, nb)]
    assert len(items) % 4 == 0
    tile_tab = jnp.asarray([i for i, _ in items], jnp.int32)
    stage_tab = jnp.asarray([st for _, st in items], jnp.int32)
    qkspec = pl.BlockSpec((seq, LANES), lambda b, hp, *_: (b, hp))
    vtspec = pl.BlockSpec((1, 1, nb, LANES, BLK), lambda b, hp, *_: (b, hp, 0, 0, 0))
    stat = pltpu.VMEM((nb, 1, 2 * BLK), F32)
    sbuf = pltpu.VMEM((G * BLK, 2 * BLK), F32)
    pbuf = pltpu.VMEM((G * BLK, 2 * BLK), BF16)
    return pl.pallas_call(
        functools.partial(_moba_kernel, n_items=len(items)),
        grid_spec=pltpu.PrefetchScalarGridSpec(
            num_scalar_prefetch=2,
            grid=(batch, npair),
            in_specs=[qkspec, qkspec, vtspec],
            out_specs=qkspec,
            scratch_shapes=[pltpu.VMEM((nb, LANES), F32), pltpu.VMEM((nb, nb, 2 * BLK), F32),
                            pltpu.VMEM((nb, 2 * BLK, LANES), BF16),
                            stat, stat, stat, pltpu.VMEM((nb, LANES, BLK), F32),
                            sbuf, sbuf, sbuf, sbuf, pbuf, pbuf, pbuf, pbuf]),
        out_shape=jax.ShapeDtypeStruct((batch * seq, MOBA_W), BF16),
        compiler_params=pltpu.CompilerParams(dimension_semantics=("arbitrary",) * 2,
                                             vmem_limit_bytes=VMEM_LIMIT),
        name="moba",
    )(tile_tab, stage_tab, mq, mk, mvt)


def _merge_kernel(x_ref, ret_ref, moba_ref, ga_ref, gb_ref, wr_ref, wm_ref, wo_ref, o_ref):
    a = _dot(ret_ref[...], wr_ref[...])
    b = _dot(moba_ref[...], wm_ref[...])
    mix = ga_ref[...].astype(F32) * a + gb_ref[...].astype(F32) * b
    o_ref[...] = x_ref[...] + _dot(mix.astype(BF16), wo_ref[...])


def _merge(x2, ret, moba, sga, sgb, w_ret_out, w_moba_out, w_o, *, tm):
    T = x2.shape[0]
    row = lambda w: pl.BlockSpec((tm, w), lambda i: (i, 0))
    return pl.pallas_call(
        _merge_kernel,
        grid=(T // tm,),
        in_specs=[row(D_MODEL), row(RET_V), row(MOBA_W), row(D_MODEL), row(D_MODEL),
                  _const_spec((RET_V, D_MODEL)), _const_spec((MOBA_W, D_MODEL)), _const_spec((D_MODEL, D_MODEL))],
        out_specs=row(D_MODEL),
        out_shape=jax.ShapeDtypeStruct((T, D_MODEL), F32),
        compiler_params=pltpu.CompilerParams(dimension_semantics=("arbitrary",), vmem_limit_bytes=VMEM_LIMIT),
        name="merge",
    )(x2, ret, moba, sga, sgb, w_ret_out.astype(BF16), w_moba_out.astype(BF16), w_o.astype(BF16))


def _ffn_kernel(x_ref, n2_ref, wg_ref, wu_ref, wd_ref, o_ref):
    x = x_ref[...]
    ms = jnp.mean(x * x, axis=-1, keepdims=True)
    h = (x * lax.rsqrt(ms + RMS_EPS) * n2_ref[...]).astype(BF16)
    g = _dot(h, wg_ref[...])
    u = _dot(h, wu_ref[...])
    act = (g * jax.nn.sigmoid(g) * u).astype(BF16)
    o_ref[...] = x + _dot(act, wd_ref[...])


def _ffn(x2, norm2_w, w_gate, w_up, w_down, *, tm):
    T = x2.shape[0]
    row = pl.BlockSpec((tm, D_MODEL), lambda i: (i, 0))
    return pl.pallas_call(
        _ffn_kernel,
        grid=(T // tm,),
        in_specs=[row, _const_spec((1, D_MODEL)), _const_spec((D_MODEL, FFN_HIDDEN)),
                  _const_spec((D_MODEL, FFN_HIDDEN)), _const_spec((FFN_HIDDEN, D_MODEL))],
        out_specs=row,
        out_shape=jax.ShapeDtypeStruct((T, D_MODEL), F32),
        compiler_params=pltpu.CompilerParams(dimension_semantics=("arbitrary",), vmem_limit_bytes=VMEM_LIMIT),
        name="ffn",
    )(x2, norm2_w.astype(F32)[None, :], w_gate.astype(BF16), w_up.astype(BF16), w_down.astype(BF16))


def kernel(x, norm1_w, w_in, q_norm_w, k_norm_w, w_ret_out, w_moba_out, w_o, norm2_w, w_ffn_gate, w_ffn_up, w_ffn_down):
    B, S, D = x.shape
    assert D == D_MODEL and S % MOBA_BLOCK == 0 and S % MERGE_ROW_TILE == 0 and S % ROW_TILE == 0
    depth = norm1_w.shape[0]
    x2 = x.reshape(B * S, D)
    for l in range(depth):
        rq, rk, rv, rgs, mq, mk, mvt, sga, sgb = _inproj(
            x2, norm1_w[l], w_in[l], q_norm_w[l], k_norm_w[l], seq=S, tm=ROW_TILE)
        ret = _retention(rq, rk, rv, rgs, batch=B, seq=S, rt=ROW_TILE)
        moba = _moba(mq, mk, mvt, batch=B, seq=S)
        x2 = _merge(x2, ret, moba, sga, sgb, w_ret_out[l], w_moba_out[l], w_o[l], tm=MERGE_ROW_TILE)
        x2 = _ffn(x2, norm2_w[l], w_ffn_gate[l], w_ffn_up[l], w_ffn_down[l], tm=ROW_TILE)
    return x2.reshape(B, S, D)
```

```python
import functools
import math

import numpy as np
import jax
import jax.numpy as jnp
from jax import lax
from jax.experimental import pallas as pl
from jax.experimental.pallas import tpu as pltpu

D_MODEL = 1024
RET_HEADS = 4
RET_DK = 128
RET_DV = 256
ROPE_BASE = 10000.0
MOBA_HEADS = 8
MOBA_DH = 64
MOBA_BLOCK = 256
MOBA_TOPK = 3
FFN_HIDDEN = 2816
RET_QK = RET_HEADS * RET_DK
RET_V = RET_HEADS * RET_DV
MOBA_W = MOBA_HEADS * MOBA_DH
IN_SPLITS = (RET_QK, RET_QK, RET_V, RET_V, MOBA_W, MOBA_W, MOBA_W, D_MODEL, D_MODEL)
IN_COLS = sum(IN_SPLITS)
RMS_EPS = 1e-6
GN_EPS = 1e-5
NEG = -1e30

LANES = 128
VMEM_LIMIT = 52 * 1024 * 1024
RET_CHUNK = 256
MOBA_STAGE = 4
ROW_TILE = 512
MERGE_ROW_TILE = 1024

F32 = jnp.float32
BF16 = jnp.bfloat16


def _dot(a, b):
    return jnp.dot(a, b, preferred_element_type=F32)


def _dot_nt(a, b):
    return lax.dot_general(a, b, (((1,), (1,)), ((), ())), preferred_element_type=F32)


def _dot_tn(a, b):
    return lax.dot_general(a, b, (((0,), (0,)), ((), ())), preferred_element_type=F32)


def _const_spec(shape):
    nd = len(shape)
    return pl.BlockSpec(shape, lambda *_: (0,) * nd, pipeline_mode=pl.Buffered(1))


def _inproj_kernel(x_ref, xn_ref, n1_ref, w_ref, wvt_ref, qn_ref, kn_ref, cos_ref, sin_ref,
                   rq_ref, rk_ref, rv_ref, rg_ref, mq_ref, mk_ref, mvt_ref, ga_ref, gb_ref, h_ref,
                   *, tm):
    i = pl.program_id(0)

    def normed(x):
        ms = jnp.mean(x * x, axis=-1, keepdims=True)
        return (x * lax.rsqrt(ms + RMS_EPS) * n1_ref[...]).astype(BF16)

    @pl.when(i == 0)
    def _():
        h_ref[0] = normed(x_ref[...])

    h = h_ref[i % 2]

    offs = np.cumsum((0,) + IN_SPLITS)

    def proj(n):
        return _dot(h, w_ref[:, int(offs[n]):int(offs[n + 1])])

    cos = cos_ref[...]
    sin = sin_ref[...]
    lane = lax.broadcasted_iota(jnp.int32, (tm, LANES), 1)

    def rotary_store(p, out_ref, scale):
        for hd in range(RET_HEADS):
            xh = p[:, hd * RET_DK:(hd + 1) * RET_DK]
            y = xh * cos + pltpu.roll(xh, RET_DK // 2, 1) * sin
            if scale is not None:
                y = y * scale
            out_ref[:, hd * RET_DK:(hd + 1) * RET_DK] = y.astype(BF16)

    rotary_store(proj(0), rq_ref, None)
    rotary_store(proj(1), rk_ref, RET_DK ** -0.5)
    rv_ref[...] = proj(2).astype(BF16)
    rg = proj(3)
    rg_ref[...] = (rg * jax.nn.sigmoid(rg)).astype(BF16)

    low_half = lane < MOBA_DH

    def head_rms_store(p, w_row_ref, out_ref, scale):
        for g in range(MOBA_W // LANES):
            cols = slice(g * LANES, (g + 1) * LANES)
            ph = p[:, cols]
            sq = ph * ph
            lo = jnp.sum(jnp.where(low_half, sq, 0.0), axis=-1, keepdims=True)
            hi = jnp.sum(jnp.where(low_half, 0.0, sq), axis=-1, keepdims=True)
            msq = jnp.where(low_half, lo, hi) * (1.0 / MOBA_DH)
            y = ph * lax.rsqrt(msq + RMS_EPS) * w_row_ref[:, cols]
            out_ref[:, cols] = (y if scale is None else y * scale).astype(BF16)

    head_rms_store(proj(4), qn_ref, mq_ref, MOBA_DH ** -0.5 * math.log2(math.e))
    head_rms_store(proj(5), kn_ref, mk_ref, None)
    mvt = _dot_nt(wvt_ref[...], h).astype(BF16)
    for pair in range(MOBA_W // LANES):
        for c in range(tm // MOBA_BLOCK):
            mvt_ref[0, pair, c] = mvt[pair * LANES:(pair + 1) * LANES, c * MOBA_BLOCK:(c + 1) * MOBA_BLOCK]
    ga_ref[...] = jax.nn.sigmoid(proj(7)).astype(BF16)
    gb_ref[...] = jax.nn.sigmoid(proj(8)).astype(BF16)
    h_ref[(i + 1) % 2] = normed(xn_ref[...])


def _inproj(x2, norm1_w, w_in, q_norm_w, k_norm_w, *, seq, tm):
    T = x2.shape[0]
    half = RET_DK // 2
    inv = ROPE_BASE ** (-np.arange(half, dtype=np.float64) / half)
    ang = np.arange(seq, dtype=np.float64)[:, None] * inv[None, :]
    cos2 = jnp.asarray(np.concatenate([np.cos(ang), np.cos(ang)], axis=1), F32)
    sin2 = jnp.asarray(np.concatenate([-np.sin(ang), np.sin(ang)], axis=1), F32)
    qn = jnp.tile(q_norm_w.astype(F32), MOBA_HEADS)[None, :]
    kn = jnp.tile(k_norm_w.astype(F32), MOBA_HEADS)[None, :]
    offs = np.cumsum((0,) + IN_SPLITS)
    w_mv_t = w_in[:, int(offs[6]):int(offs[7])].T.astype(BF16)
    n_tiles = T // tm
    nt = seq // tm
    npair = MOBA_W // LANES
    nb = seq // MOBA_BLOCK
    row = lambda w: pl.BlockSpec((tm, w), lambda i: (i, 0))
    next_row = pl.BlockSpec((tm, D_MODEL), lambda i: (jnp.minimum(i + 1, n_tiles - 1), 0))
    pos_row = pl.BlockSpec((tm, LANES), lambda i: (i % nt, 0))
    row_out = lambda w: (row(w), jax.ShapeDtypeStruct((T, w), BF16))
    mvt_out = (pl.BlockSpec((1, npair, tm // MOBA_BLOCK, LANES, MOBA_BLOCK), lambda i: (i // nt, 0, i % nt, 0, 0)),
               jax.ShapeDtypeStruct((T // seq, npair, nb, LANES, MOBA_BLOCK), BF16))
    outs = [row_out(RET_QK), row_out(RET_QK), row_out(RET_V), row_out(RET_V), row_out(MOBA_W), row_out(MOBA_W),
            mvt_out, row_out(D_MODEL), row_out(D_MODEL)]
    return pl.pallas_call(
        functools.partial(_inproj_kernel, tm=tm),
        grid=(n_tiles,),
        in_specs=[row(D_MODEL), next_row, _const_spec((1, D_MODEL)), _const_spec((D_MODEL, IN_COLS)),
                  _const_spec((MOBA_W, D_MODEL)),
                  _const_spec((1, MOBA_W)), _const_spec((1, MOBA_W)), pos_row, pos_row],
        out_specs=[o[0] for o in outs],
        out_shape=[o[1] for o in outs],
        scratch_shapes=[pltpu.VMEM((2, tm, D_MODEL), BF16)],
        compiler_params=pltpu.CompilerParams(dimension_semantics=("arbitrary",), vmem_limit_bytes=VMEM_LIMIT),
        name="inproj",
    )(x2, x2, norm1_w.astype(F32)[None, :], w_in.astype(BF16), w_mv_t, qn, kn, cos2, sin2)


def _retention_kernel(q_ref, k_ref, v_ref, g_ref, dmat_ref, xi_ref, zeta_ref, o_ref, state_ref,
                      *, rt, chunk, g_chunk):
    @pl.when(pl.program_id(1) == 0)
    def _():
        state_ref[...] = jnp.zeros_like(state_ref)

    for c in range(rt // chunk):
        rows = slice(c * chunk, (c + 1) * chunk)
        for hd in range(RET_HEADS):
            kcols = slice(hd * RET_DK, (hd + 1) * RET_DK)
            vcols = slice(hd * RET_DV, (hd + 1) * RET_DV)
            q = q_ref[rows, kcols]
            k = k_ref[rows, kcols]
            v = v_ref[rows, vcols]
            state = state_ref[hd]
            scores = _dot_nt(q, k) * dmat_ref[hd]
            q_dec = (q.astype(F32) * xi_ref[hd]).astype(BF16)
            o = _dot(scores.astype(BF16), v) + _dot(q_dec, state.astype(BF16))
            k_dec = (k.astype(F32) * zeta_ref[hd]).astype(BF16)
            state_ref[hd] = g_chunk[hd] * state + _dot_tn(k_dec, v)
            mu = jnp.mean(o, axis=-1, keepdims=True)
            d = o - mu
            var = jnp.mean(d * d, axis=-1, keepdims=True)
            y = d * lax.rsqrt(var + GN_EPS)
            o_ref[rows, vcols] = (g_ref[rows, vcols].astype(F32) * y).astype(BF16)


def _retention(rq, rk, rv, rgs, *, batch, seq, rt):
    C = RET_CHUNK
    lg = np.log1p(-np.exp2(-5.0 - np.arange(RET_HEADS, dtype=np.float64)))
    idx = np.arange(C, dtype=np.float64)
    diff = idx[:, None] - idx[None, :]
    dmat = np.where(diff >= 0, np.exp(np.maximum(diff, 0.0)[None] * lg[:, None, None]), 0.0)
    xi = np.broadcast_to(np.exp((idx + 1.0)[None, :] * lg[:, None])[:, :, None], (RET_HEADS, C, LANES))
    zeta = np.broadcast_to(np.exp((C - 1.0 - idx)[None, :] * lg[:, None])[:, :, None], (RET_HEADS, C, LANES))
    g_chunk = tuple(float(v) for v in np.exp(C * lg))
    nt = seq // rt
    row = lambda w: pl.BlockSpec((rt, w), lambda b, j: (b * nt + j, 0))
    return pl.pallas_call(
        functools.partial(_retention_kernel, rt=rt, chunk=C, g_chunk=g_chunk),
        grid=(batch, nt),
        in_specs=[row(RET_QK), row(RET_QK), row(RET_V), row(RET_V),
                  _const_spec((RET_HEADS, C, C)), _const_spec((RET_HEADS, C, LANES)),
                  _const_spec((RET_HEADS, C, LANES))],
        out_specs=row(RET_V),
        out_shape=jax.ShapeDtypeStruct((batch * seq, RET_V), BF16),
        scratch_shapes=[pltpu.VMEM((RET_HEADS, RET_DK, RET_DV), F32)],
        compiler_params=pltpu.CompilerParams(dimension_semantics=("arbitrary", "arbitrary"),
                                             vmem_limit_bytes=VMEM_LIMIT),
        name="retention",
    )(rq, rk, rv, rgs, jnp.asarray(dmat, F32), jnp.asarray(xi, F32), jnp.asarray(zeta, F32))


def _moba_kernel(tile_tab_ref, stage_tab_ref, q_ref, k_ref, vt_ref, o_ref,
                 kbar_ref, bias_ref, qcat_ref, m_ref, r_ref, l_ref, acc_ref,
                 s0_ref, s1_ref, s2_ref, s3_ref, p0_ref, p1_ref, p2_ref, p3_ref, *, n_items):
    BLK = MOBA_BLOCK
    DH = MOBA_DH
    G = MOBA_STAGE
    STG = G * BLK
    nb = vt_ref.shape[2]
    nbp = kbar_ref.shape[0]
    s_bufs = (s0_ref, s1_ref, s2_ref, s3_ref)
    p_bufs = (p0_ref, p1_ref, p2_ref, p3_ref)
    lane = lax.broadcasted_iota(jnp.int32, (1, LANES), 1)

    def q_cat_body(i, c):
        q_pair = q_ref[pl.ds(pl.multiple_of(i * BLK, BLK), BLK), :]
        zero = jnp.zeros_like(q_pair)
        qcat_ref[i, :BLK, :] = jnp.where(lane < DH, q_pair, zero)
        qcat_ref[i, BLK:, :] = jnp.where(lane >= DH, q_pair, zero)
        return c

    lax.fori_loop(0, nb, q_cat_body, 0)

    def q_cat_of(i):
        return qcat_ref[i]

    kbar_ref[...] = jnp.zeros_like(kbar_ref)

    def kbar_body(n, c):
        kb = k_ref[pl.ds(pl.multiple_of(n * BLK, BLK), BLK), :]
        kbar_ref[pl.ds(n, 1), :] = jnp.mean(kb.astype(F32), axis=0, keepdims=True)
        return c

    lax.fori_loop(0, nb, kbar_body, 0)

    kbar = kbar_ref[...]
    kbar_hi = kbar.astype(BF16)
    kbar_lo = (kbar - kbar_hi.astype(F32)).astype(BF16)
    blk_idx = lax.broadcasted_iota(jnp.int32, (nbp, 2 * BLK), 0)

    def select_blocks(i):
        q_cat = q_cat_of(i)
        gate = _dot_nt(kbar_hi, q_cat) + _dot_nt(kbar_lo, q_cat)
        past = blk_idx < i
        gate = jnp.where(past, gate, NEG)
        sel = jnp.zeros((nbp, 2 * BLK), jnp.bool_)
        for _ in range(MOBA_TOPK):
            best = jnp.max(gate, axis=0, keepdims=True)
            first = jnp.min(jnp.where(gate == best, blk_idx, nbp), axis=0, keepdims=True)
            pick = blk_idx == first
            sel = sel | (pick & past)
            gate = jnp.where(pick, -3e38, gate)
        bias_ref[i] = jnp.where(sel, 0.0, NEG)

    def select_body(u, c):
        select_blocks(2 * u)
        select_blocks(2 * u + 1)
        return c

    lax.fori_loop(0, nb // 2, select_body, 0)

    m_ref[...] = jnp.full(m_ref.shape, NEG, F32)
    r_ref[...] = jnp.full(r_ref.shape, NEG, F32)
    l_ref[...] = jnp.zeros_like(l_ref)
    acc_ref[...] = jnp.zeros_like(acc_ref)

    def scores(i, st, dst_ref):
        ks = k_ref[pl.ds(pl.multiple_of(st * STG, STG), STG), :]
        dst_ref[...] = _dot_nt(ks, q_cat_of(i))

    def values(st, p_ref):
        vt = jnp.concatenate([vt_ref[0, 0, G * st + g] for g in range(G)], axis=1)
        pv = _dot(jnp.concatenate([vt, jnp.ones((16, STG), BF16)], axis=0), p_ref[...])
        return pv[:DH, :BLK], pv[DH:2 * DH, BLK:], pv[2 * DH:2 * DH + 1, :]

    def fold(pv, m_pv, i):
        num0, num1, sums = pv
        scale = jnp.exp2(r_ref[i] - m_pv)
        r_ref[i] = m_pv
        l_ref[i] = scale * l_ref[i] + sums
        acc_ref[i, :DH, :] = scale[:, :BLK] * acc_ref[i, :DH, :] + num0
        acc_ref[i, DH:, :] = scale[:, BLK:] * acc_ref[i, DH:, :] + num1

    key_idx = lax.broadcasted_iota(jnp.int32, (BLK, 2 * BLK), 0)
    qry_idx = lax.broadcasted_iota(jnp.int32, (BLK, 2 * BLK), 1) & (BLK - 1)
    causal = key_idx <= qry_idx

    def make_probs(own):
        def probs(i, st, s_ref, p_ref, tok):
            n_live = G if own is None else own + 1
            tiles, shifts = [], []
            m_new = m_ref[i]
            for g in range(n_live):
                sg = s_ref[g * BLK:(g + 1) * BLK, :]
                if g == own:
                    sg = jnp.where(causal, sg, NEG)
                    bg = None
                    m_new = jnp.maximum(m_new, jnp.max(sg, axis=0, keepdims=True))
                else:
                    bg = bias_ref[i, pl.ds(G * st + g, 1), :]
                    m_new = jnp.maximum(m_new, jnp.max(sg, axis=0, keepdims=True) + bg)
                tiles.append(sg)
                shifts.append(bg)
            m_ref[i] = m_new
            m_new = m_new + tok
            for g in range(G):
                rows = slice(g * BLK, (g + 1) * BLK)
                if g >= n_live:
                    p_ref[rows, :] = jnp.zeros((BLK, 2 * BLK), BF16)
                else:
                    shift = m_new if shifts[g] is None else m_new - shifts[g]
                    p_ref[rows, :] = jnp.exp2(tiles[g] - shift).astype(BF16)
            return m_new
        return probs

    def run_items(count, item, probs, tok0):
        def clamped(t):
            return item(jnp.clip(t, 0, count - 1))

        scores(*clamped(0), s_bufs[0])
        scores(*clamped(1), s_bufs[1])

        def slot(t, k, carry):
            m_prev, toks = carry
            scores(*clamped(t + 2), s_bufs[(k + 2) % 4])
            i_prev, st_prev = clamped(t - 1)
            pv = values(st_prev, p_bufs[(k - 1) % 4])
            i, st = clamped(t)
            m_t = probs(i, st, s_bufs[k % 4], p_bufs[k % 4], toks[0])
            fold(pv, m_prev, i_prev)
            return m_t, (toks[1], toks[2], pv[2] * 0.0)

        unroll = 4

        def body(it, carry):
            for k in range(unroll):
                carry = slot(unroll * it + k, k, carry)
            return carry

        m_last, _ = lax.fori_loop(0, count // unroll, body, (r_ref[clamped(0)[0]], (tok0, tok0, tok0)))
        i_last, st_last = clamped(count - 1)
        pv_last = values(st_last, p_bufs[(count - 1) % 4])
        fold(pv_last, m_last, i_last)
        tok_last = pv_last[2] * 0.0
        p_bufs[3][...] = jnp.broadcast_to(tok_last.astype(BF16), p_bufs[3].shape)
        return tok_last

    p_bufs[3][...] = jnp.zeros_like(p_bufs[3])
    tok = jnp.zeros((1, 2 * BLK), F32)
    for own in range(G):
        tok = run_items(nb // G, lambda u, own=own: (G * u + own, u), make_probs(own), tok)
    run_items(n_items, lambda t: (tile_tab_ref[t], stage_tab_ref[t]), make_probs(None), tok)

    def finish(i, c):
        l = l_ref[i]
        acc = acc_ref[i]
        out_t = jnp.concatenate([acc[:DH] / l[:, :BLK], acc[DH:] / l[:, BLK:]], axis=0)
        o_ref[pl.ds(pl.multiple_of(i * BLK, BLK), BLK), :] = out_t.T.astype(BF16)
        return c

    lax.fori_loop(0, nb, finish, 0)


def _moba(mq, mk, mvt, *, batch, seq):
    BLK = MOBA_BLOCK
    nb = seq // BLK
    G = MOBA_STAGE
    assert nb % (4 * G) == 0
    npair = MOBA_W // LANES
    items = [(i, st) for st in range(nb // G) for i in range(G * st + G, nb)]
    assert len(items) % 4 == 0
    tile_tab = jnp.asarray([i for i, _ in items], jnp.int32)
    stage_tab = jnp.asarray([st for _, st in items], jnp.int32)
    qkspec = pl.BlockSpec((seq, LANES), lambda b, hp, *_: (b, hp))
    vtspec = pl.BlockSpec((1, 1, nb, LANES, BLK), lambda b, hp, *_: (b, hp, 0, 0, 0))
    stat = pltpu.VMEM((nb, 1, 2 * BLK), F32)
    sbuf = pltpu.VMEM((G * BLK, 2 * BLK), F32)
    pbuf = pltpu.VMEM((G * BLK, 2 * BLK), BF16)
    return pl.pallas_call(
        functools.partial(_moba_kernel, n_items=len(items)),
        grid_spec=pltpu.PrefetchScalarGridSpec(
            num_scalar_prefetch=2,
            grid=(batch, npair),
            in_specs=[qkspec, qkspec, vtspec],
            out_specs=qkspec,
            scratch_shapes=[pltpu.VMEM((nb, LANES), F32), pltpu.VMEM((nb, nb, 2 * BLK), F32),
                            pltpu.VMEM((nb, 2 * BLK, LANES), BF16),
                            stat, stat, stat, pltpu.VMEM((nb, LANES, BLK), F32),
                            sbuf, sbuf, sbuf, sbuf, pbuf, pbuf, pbuf, pbuf]),
        out_shape=jax.ShapeDtypeStruct((batch * seq, MOBA_W), BF16),
        compiler_params=pltpu.CompilerParams(dimension_semantics=("arbitrary",) * 2,
                                             vmem_limit_bytes=VMEM_LIMIT),
        name="moba",
    )(tile_tab, stage_tab, mq, mk, mvt)


def _merge_kernel(x_ref, ret_ref, moba_ref, ga_ref, gb_ref, wr_ref, wm_ref, wo_ref, o_ref):
    a = _dot(ret_ref[...], wr_ref[...])
    b = _dot(moba_ref[...], wm_ref[...])
    mix = ga_ref[...].astype(F32) * a + gb_ref[...].astype(F32) * b
    o_ref[...] = x_ref[...] + _dot(mix.astype(BF16), wo_ref[...])


def _merge(x2, ret, moba, sga, sgb, w_ret_out, w_moba_out, w_o, *, tm):
    T = x2.shape[0]
    row = lambda w: pl.BlockSpec((tm, w), lambda i: (i, 0))
    return pl.pallas_call(
        _merge_kernel,
        grid=(T // tm,),
        in_specs=[row(D_MODEL), row(RET_V), row(MOBA_W), row(D_MODEL), row(D_MODEL),
                  _const_spec((RET_V, D_MODEL)), _const_spec((MOBA_W, D_MODEL)), _const_spec((D_MODEL, D_MODEL))],
        out_specs=row(D_MODEL),
        out_shape=jax.ShapeDtypeStruct((T, D_MODEL), F32),
        compiler_params=pltpu.CompilerParams(dimension_semantics=("arbitrary",), vmem_limit_bytes=VMEM_LIMIT),
        name="merge",
    )(x2, ret, moba, sga, sgb, w_ret_out.astype(BF16), w_moba_out.astype(BF16), w_o.astype(BF16))


def _ffn_kernel(x_ref, n2_ref, wg_ref, wu_ref, wd_ref, o_ref):
    x = x_ref[...]
    ms = jnp.mean(x * x, axis=-1, keepdims=True)
    h = (x * lax.rsqrt(ms + RMS_EPS) * n2_ref[...]).astype(BF16)
    g = _dot(h, wg_ref[...])
    u = _dot(h, wu_ref[...])
    act = (g * jax.nn.sigmoid(g) * u).astype(BF16)
    o_ref[...] = x + _dot(act, wd_ref[...])


def _ffn(x2, norm2_w, w_gate, w_up, w_down, *, tm):
    T = x2.shape[0]
    row = pl.BlockSpec((tm, D_MODEL), lambda i: (i, 0))
    return pl.pallas_call(
        _ffn_kernel,
        grid=(T // tm,),
        in_specs=[row, _const_spec((1, D_MODEL)), _const_spec((D_MODEL, FFN_HIDDEN)),
                  _const_spec((D_MODEL, FFN_HIDDEN)), _const_spec((FFN_HIDDEN, D_MODEL))],
        out_specs=row,
        out_shape=jax.ShapeDtypeStruct((T, D_MODEL), F32),
        compiler_params=pltpu.CompilerParams(dimension_semantics=("arbitrary",), vmem_limit_bytes=VMEM_LIMIT),
        name="ffn",
    )(x2, norm2_w.astype(F32)[None, :], w_gate.astype(BF16), w_up.astype(BF16), w_down.astype(BF16))


def kernel(x, norm1_w, w_in, q_norm_w, k_norm_w, w_ret_out, w_moba_out, w_o, norm2_w, w_ffn_gate, w_ffn_up, w_ffn_down):
    B, S, D = x.shape
    assert D == D_MODEL and S % MOBA_BLOCK == 0 and S % MERGE_ROW_TILE == 0 and S % ROW_TILE == 0
    depth = norm1_w.shape[0]
    x2 = x.reshape(B * S, D)
    for l in range(depth):
        rq, rk, rv, rgs, mq, mk, mvt, sga, sgb = _inproj(
            x2, norm1_w[l], w_in[l], q_norm_w[l], k_norm_w[l], seq=S, tm=ROW_TILE)
        ret = _retention(rq, rk, rv, rgs, batch=B, seq=S, rt=ROW_TILE)
        moba = _moba(mq, mk, mvt, batch=B, seq=S)
        x2 = _merge(x2, ret, moba, sga, sgb, w_ret_out[l], w_moba_out[l], w_o[l], tm=MERGE_ROW_TILE)
        x2 = _ffn(x2, norm2_w[l], w_ffn_gate[l], w_ffn_up[l], w_ffn_down[l], tm=ROW_TILE)
    return x2.reshape(B, S, D)
```

```python
import functools
import math

import numpy as np
import jax
import jax.numpy as jnp
from jax import lax
from jax.experimental import pallas as pl
from jax.experimental.pallas import tpu as pltpu

D_MODEL = 1024
RET_HEADS = 4
RET_DK = 128
RET_DV = 256
ROPE_BASE = 10000.0
MOBA_HEADS = 8
MOBA_DH = 64
MOBA_BLOCK = 256
MOBA_TOPK = 3
FFN_HIDDEN = 2816
RET_QK = RET_HEADS * RET_DK
RET_V = RET_HEADS * RET_DV
MOBA_W = MOBA_HEADS * MOBA_DH
IN_SPLITS = (RET_QK, RET_QK, RET_V, RET_V, MOBA_W, MOBA_W, MOBA_W, D_MODEL, D_MODEL)
IN_COLS = sum(IN_SPLITS)
RMS_EPS = 1e-6
GN_EPS = 1e-5
NEG = -1e30

LANES = 128
VMEM_LIMIT = 52 * 1024 * 1024
RET_CHUNK = 256
MOBA_STAGE = 4
ROW_TILE = 512
MERGE_ROW_TILE = 1024

F32 = jnp.float32
BF16 = jnp.bfloat16


def _dot(a, b):
    return jnp.dot(a, b, preferred_element_type=F32)


def _dot_nt(a, b):
    return lax.dot_general(a, b, (((1,), (1,)), ((), ())), preferred_element_type=F32)


def _dot_tn(a, b):
    return lax.dot_general(a, b, (((0,), (0,)), ((), ())), preferred_element_type=F32)


def _const_spec(shape):
    nd = len(shape)
    return pl.BlockSpec(shape, lambda *_: (0,) * nd, pipeline_mode=pl.Buffered(1))


def _inproj_kernel(x_ref, xn_ref, n1_ref, w_ref, qn_ref, kn_ref, cos_ref, sin_ref,
                   rq_ref, rk_ref, rv_ref, rg_ref, mq_ref, mk_ref, mvt_ref, ga_ref, gb_ref, h_ref,
                   *, tm):
    i = pl.program_id(0)

    def normed(x):
        ms = jnp.mean(x * x, axis=-1, keepdims=True)
        return (x * lax.rsqrt(ms + RMS_EPS) * n1_ref[...]).astype(BF16)

    @pl.when(i == 0)
    def _():
        h_ref[0] = normed(x_ref[...])

    h = h_ref[i % 2]

    offs = np.cumsum((0,) + IN_SPLITS)

    def proj(n):
        return _dot(h, w_ref[:, int(offs[n]):int(offs[n + 1])])

    cos = cos_ref[...]
    sin = sin_ref[...]
    lane = lax.broadcasted_iota(jnp.int32, (tm, LANES), 1)

    def rotary_store(p, out_ref, scale):
        for hd in range(RET_HEADS):
            xh = p[:, hd * RET_DK:(hd + 1) * RET_DK]
            y = xh * cos + pltpu.roll(xh, RET_DK // 2, 1) * sin
            if scale is not None:
                y = y * scale
            out_ref[:, hd * RET_DK:(hd + 1) * RET_DK] = y.astype(BF16)

    rotary_store(proj(0), rq_ref, None)
    rotary_store(proj(1), rk_ref, RET_DK ** -0.5)
    rv_ref[...] = proj(2).astype(BF16)
    rg = proj(3)
    rg_ref[...] = (rg * jax.nn.sigmoid(rg)).astype(BF16)

    low_half = lane < MOBA_DH

    def head_rms_store(p, w_row_ref, out_ref, scale):
        for g in range(MOBA_W // LANES):
            cols = slice(g * LANES, (g + 1) * LANES)
            ph = p[:, cols]
            sq = ph * ph
            lo = jnp.sum(jnp.where(low_half, sq, 0.0), axis=-1, keepdims=True)
            hi = jnp.sum(jnp.where(low_half, 0.0, sq), axis=-1, keepdims=True)
            msq = jnp.where(low_half, lo, hi) * (1.0 / MOBA_DH)
            y = ph * lax.rsqrt(msq + RMS_EPS) * w_row_ref[:, cols]
            out_ref[:, cols] = (y if scale is None else y * scale).astype(BF16)

    head_rms_store(proj(4), qn_ref, mq_ref, MOBA_DH ** -0.5 * math.log2(math.e))
    head_rms_store(proj(5), kn_ref, mk_ref, None)
    mvt = proj(6).T.astype(BF16)
    for pair in range(MOBA_W // LANES):
        for c in range(tm // MOBA_BLOCK):
            mvt_ref[0, pair, c] = mvt[pair * LANES:(pair + 1) * LANES, c * MOBA_BLOCK:(c + 1) * MOBA_BLOCK]
    ga_ref[...] = jax.nn.sigmoid(proj(7)).astype(BF16)
    gb_ref[...] = jax.nn.sigmoid(proj(8)).astype(BF16)
    h_ref[(i + 1) % 2] = normed(xn_ref[...])


def _inproj(x2, norm1_w, w_in, q_norm_w, k_norm_w, *, seq, tm):
    T = x2.shape[0]
    half = RET_DK // 2
    inv = ROPE_BASE ** (-np.arange(half, dtype=np.float64) / half)
    ang = np.arange(seq, dtype=np.float64)[:, None] * inv[None, :]
    cos2 = jnp.asarray(np.concatenate([np.cos(ang), np.cos(ang)], axis=1), F32)
    sin2 = jnp.asarray(np.concatenate([-np.sin(ang), np.sin(ang)], axis=1), F32)
    qn = jnp.tile(q_norm_w.astype(F32), MOBA_HEADS)[None, :]
    kn = jnp.tile(k_norm_w.astype(F32), MOBA_HEADS)[None, :]
    offs = np.cumsum((0,) + IN_SPLITS)
    n_tiles = T // tm
    nt = seq // tm
    npair = MOBA_W // LANES
    nb = seq // MOBA_BLOCK
    row = lambda w: pl.BlockSpec((tm, w), lambda i: (i, 0))
    next_row = pl.BlockSpec((tm, D_MODEL), lambda i: (jnp.minimum(i + 1, n_tiles - 1), 0))
    pos_row = pl.BlockSpec((tm, LANES), lambda i: (i % nt, 0))
    row_out = lambda w: (row(w), jax.ShapeDtypeStruct((T, w), BF16))
    mvt_out = (pl.BlockSpec((1, npair, tm // MOBA_BLOCK, LANES, MOBA_BLOCK), lambda i: (i // nt, 0, i % nt, 0, 0)),
               jax.ShapeDtypeStruct((T // seq, npair, nb, LANES, MOBA_BLOCK), BF16))
    outs = [row_out(RET_QK), row_out(RET_QK), row_out(RET_V), row_out(RET_V), row_out(MOBA_W), row_out(MOBA_W),
            mvt_out, row_out(D_MODEL), row_out(D_MODEL)]
    return pl.pallas_call(
        functools.partial(_inproj_kernel, tm=tm),
        grid=(n_tiles,),
        in_specs=[row(D_MODEL), next_row, _const_spec((1, D_MODEL)), _const_spec((D_MODEL, IN_COLS)),
                  _const_spec((1, MOBA_W)), _const_spec((1, MOBA_W)), pos_row, pos_row],
        out_specs=[o[0] for o in outs],
        out_shape=[o[1] for o in outs],
        scratch_shapes=[pltpu.VMEM((2, tm, D_MODEL), BF16)],
        compiler_params=pltpu.CompilerParams(dimension_semantics=("arbitrary",), vmem_limit_bytes=VMEM_LIMIT),
        name="inproj",
    )(x2, x2, norm1_w.astype(F32)[None, :], w_in.astype(BF16), qn, kn, cos2, sin2)


def _retention_kernel(q_ref, k_ref, v_ref, g_ref, dmat_ref, xi_ref, zeta_ref, o_ref, state_ref,
                      *, rt, chunk, g_chunk):
    @pl.when(pl.program_id(1) == 0)
    def _():
        state_ref[...] = jnp.zeros_like(state_ref)

    for c in range(rt // chunk):
        rows = slice(c * chunk, (c + 1) * chunk)
        for hd in range(RET_HEADS):
            kcols = slice(hd * RET_DK, (hd + 1) * RET_DK)
            vcols = slice(hd * RET_DV, (hd + 1) * RET_DV)
            q = q_ref[rows, kcols]
            k = k_ref[rows, kcols]
            v = v_ref[rows, vcols]
            state = state_ref[hd]
            scores = _dot_nt(q, k) * dmat_ref[hd]
            q_dec = (q.astype(F32) * xi_ref[hd]).astype(BF16)
            o = _dot(scores.astype(BF16), v) + _dot(q_dec, state.astype(BF16))
            k_dec = (k.astype(F32) * zeta_ref[hd]).astype(BF16)
            state_ref[hd] = g_chunk[hd] * state + _dot_tn(k_dec, v)
            mu = jnp.mean(o, axis=-1, keepdims=True)
            d = o - mu
            var = jnp.mean(d * d, axis=-1, keepdims=True)
            y = d * lax.rsqrt(var + GN_EPS)
            o_ref[rows, vcols] = (g_ref[rows, vcols].astype(F32) * y).astype(BF16)


def _retention(rq, rk, rv, rgs, *, batch, seq, rt):
    C = RET_CHUNK
    lg = np.log1p(-np.exp2(-5.0 - np.arange(RET_HEADS, dtype=np.float64)))
    idx = np.arange(C, dtype=np.float64)
    diff = idx[:, None] - idx[None, :]
    dmat = np.where(diff >= 0, np.exp(np.maximum(diff, 0.0)[None] * lg[:, None, None]), 0.0)
    xi = np.broadcast_to(np.exp((idx + 1.0)[None, :] * lg[:, None])[:, :, None], (RET_HEADS, C, LANES))
    zeta = np.broadcast_to(np.exp((C - 1.0 - idx)[None, :] * lg[:, None])[:, :, None], (RET_HEADS, C, LANES))
    g_chunk = tuple(float(v) for v in np.exp(C * lg))
    nt = seq // rt
    row = lambda w: pl.BlockSpec((rt, w), lambda b, j: (b * nt + j, 0))
    return pl.pallas_call(
        functools.partial(_retention_kernel, rt=rt, chunk=C, g_chunk=g_chunk),
        grid=(batch, nt),
        in_specs=[row(RET_QK), row(RET_QK), row(RET_V), row(RET_V),
                  _const_spec((RET_HEADS, C, C)), _const_spec((RET_HEADS, C, LANES)),
                  _const_spec((RET_HEADS, C, LANES))],
        out_specs=row(RET_V),
        out_shape=jax.ShapeDtypeStruct((batch * seq, RET_V), BF16),
        scratch_shapes=[pltpu.VMEM((RET_HEADS, RET_DK, RET_DV), F32)],
        compiler_params=pltpu.CompilerParams(dimension_semantics=("arbitrary", "arbitrary"),
                                             vmem_limit_bytes=VMEM_LIMIT),
        name="retention",
    )(rq, rk, rv, rgs, jnp.asarray(dmat, F32), jnp.asarray(xi, F32), jnp.asarray(zeta, F32))


def _moba_kernel(tile_tab_ref, stage_tab_ref, q_ref, k_ref, vt_ref, o_ref,
                 kbar_ref, bias_ref, qcat_ref, m_ref, r_ref, l_ref, acc_ref,
                 s0_ref, s1_ref, s2_ref, s3_ref, p0_ref, p1_ref, p2_ref, p3_ref, *, n_items):
    BLK = MOBA_BLOCK
    DH = MOBA_DH
    G = MOBA_STAGE
    STG = G * BLK
    nb = vt_ref.shape[2]
    nbp = kbar_ref.shape[0]
    s_bufs = (s0_ref, s1_ref, s2_ref, s3_ref)
    p_bufs = (p0_ref, p1_ref, p2_ref, p3_ref)
    lane = lax.broadcasted_iota(jnp.int32, (1, LANES), 1)

    def q_cat_body(i, c):
        q_pair = q_ref[pl.ds(pl.multiple_of(i * BLK, BLK), BLK), :]
        zero = jnp.zeros_like(q_pair)
        qcat_ref[i, :BLK, :] = jnp.where(lane < DH, q_pair, zero)
        qcat_ref[i, BLK:, :] = jnp.where(lane >= DH, q_pair, zero)
        return c

    lax.fori_loop(0, nb, q_cat_body, 0)

    def q_cat_of(i):
        return qcat_ref[i]

    kbar_ref[...] = jnp.zeros_like(kbar_ref)

    def kbar_body(n, c):
        kb = k_ref[pl.ds(pl.multiple_of(n * BLK, BLK), BLK), :]
        kbar_ref[pl.ds(n, 1), :] = jnp.mean(kb.astype(F32), axis=0, keepdims=True)
        return c

    lax.fori_loop(0, nb, kbar_body, 0)

    kbar = kbar_ref[...]
    kbar_hi = kbar.astype(BF16)
    kbar_lo = (kbar - kbar_hi.astype(F32)).astype(BF16)
    blk_idx = lax.broadcasted_iota(jnp.int32, (nbp, 2 * BLK), 0)

    def select_blocks(i):
        q_cat = q_cat_of(i)
        gate = _dot_nt(kbar_hi, q_cat) + _dot_nt(kbar_lo, q_cat)
        past = blk_idx < i
        gate = jnp.where(past, gate, NEG)
        sel = jnp.zeros((nbp, 2 * BLK), jnp.bool_)
        for _ in range(MOBA_TOPK):
            best = jnp.max(gate, axis=0, keepdims=True)
            first = jnp.min(jnp.where(gate == best, blk_idx, nbp), axis=0, keepdims=True)
            pick = blk_idx == first
            sel = sel | (pick & past)
            gate = jnp.where(pick, -3e38, gate)
        bias_ref[i] = jnp.where(sel, 0.0, NEG)

    def select_body(u, c):
        select_blocks(2 * u)
        select_blocks(2 * u + 1)
        return c

    lax.fori_loop(0, nb // 2, select_body, 0)

    m_ref[...] = jnp.full(m_ref.shape, NEG, F32)
    r_ref[...] = jnp.full(r_ref.shape, NEG, F32)
    l_ref[...] = jnp.zeros_like(l_ref)
    acc_ref[...] = jnp.zeros_like(acc_ref)

    def scores(i, st, dst_ref):
        ks = k_ref[pl.ds(pl.multiple_of(st * STG, STG), STG), :]
        dst_ref[...] = _dot_nt(ks, q_cat_of(i))

    def values(st, p_ref):
        vt = jnp.concatenate([vt_ref[0, 0, G * st + g] for g in range(G)], axis=1)
        pv = _dot(jnp.concatenate([vt, jnp.ones((16, STG), BF16)], axis=0), p_ref[...])
        return pv[:DH, :BLK], pv[DH:2 * DH, BLK:], pv[2 * DH:2 * DH + 1, :]

    def fold(pv, m_pv, i):
        num0, num1, sums = pv
        scale = jnp.exp2(r_ref[i] - m_pv)
        r_ref[i] = m_pv
        l_ref[i] = scale * l_ref[i] + sums
        acc_ref[i, :DH, :] = scale[:, :BLK] * acc_ref[i, :DH, :] + num0
        acc_ref[i, DH:, :] = scale[:, BLK:] * acc_ref[i, DH:, :] + num1

    key_idx = lax.broadcasted_iota(jnp.int32, (BLK, 2 * BLK), 0)
    qry_idx = lax.broadcasted_iota(jnp.int32, (BLK, 2 * BLK), 1) & (BLK - 1)
    causal = key_idx <= qry_idx

    def make_probs(own):
        def probs(i, st, s_ref, p_ref, tok):
            n_live = G if own is None else own + 1
            tiles, shifts = [], []
            m_new = m_ref[i]
            for g in range(n_live):
                sg = s_ref[g * BLK:(g + 1) * BLK, :]
                if g == own:
                    sg = jnp.where(causal, sg, NEG)
                    bg = None
                    m_new = jnp.maximum(m_new, jnp.max(sg, axis=0, keepdims=True))
                else:
                    bg = bias_ref[i, pl.ds(G * st + g, 1), :]
                    m_new = jnp.maximum(m_new, jnp.max(sg, axis=0, keepdims=True) + bg)
                tiles.append(sg)
                shifts.append(bg)
            m_ref[i] = m_new
            m_new = m_new + tok
            for g in range(G):
                rows = slice(g * BLK, (g + 1) * BLK)
                if g >= n_live:
                    p_ref[rows, :] = jnp.zeros((BLK, 2 * BLK), BF16)
                else:
                    shift = m_new if shifts[g] is None else m_new - shifts[g]
                    p_ref[rows, :] = jnp.exp2(tiles[g] - shift).astype(BF16)
            return m_new
        return probs

    def run_items(count, item, probs, tok0):
        def clamped(t):
            return item(jnp.clip(t, 0, count - 1))

        scores(*clamped(0), s_bufs[0])
        scores(*clamped(1), s_bufs[1])

        def slot(t, k, carry):
            m_prev, toks = carry
            scores(*clamped(t + 2), s_bufs[(k + 2) % 4])
            i_prev, st_prev = clamped(t - 1)
            pv = values(st_prev, p_bufs[(k - 1) % 4])
            i, st = clamped(t)
            m_t = probs(i, st, s_bufs[k % 4], p_bufs[k % 4], toks[0])
            fold(pv, m_prev, i_prev)
            return m_t, (toks[1], toks[2], pv[2] * 0.0)

        unroll = 4

        def body(it, carry):
            for k in range(unroll):
                carry = slot(unroll * it + k, k, carry)
            return carry

        m_last, _ = lax.fori_loop(0, count // unroll, body, (r_ref[clamped(0)[0]], (tok0, tok0, tok0)))
        i_last, st_last = clamped(count - 1)
        pv_last = values(st_last, p_bufs[(count - 1) % 4])
        fold(pv_last, m_last, i_last)
        tok_last = pv_last[2] * 0.0
        p_bufs[3][...] = jnp.broadcast_to(tok_last.astype(BF16), p_bufs[3].shape)
        return tok_last

    p_bufs[3][...] = jnp.zeros_like(p_bufs[3])
    tok = jnp.zeros((1, 2 * BLK), F32)
    for own in range(G):
        tok = run_items(nb // G, lambda u, own=own: (G * u + own, u), make_probs(own), tok)
    run_items(n_items, lambda t: (tile_tab_ref[t], stage_tab_ref[t]), make_probs(None), tok)

    def finish(i, c):
        l = l_ref[i]
        acc = acc_ref[i]
        out_t = jnp.concatenate([acc[:DH] / l[:, :BLK], acc[DH:] / l[:, BLK:]], axis=0)
        o_ref[pl.ds(pl.multiple_of(i * BLK, BLK), BLK), :] = out_t.T.astype(BF16)
        return c

    lax.fori_loop(0, nb, finish, 0)


def _moba(mq, mk, mvt, *, batch, seq):
    BLK = MOBA_BLOCK
    nb = seq // BLK
    G = MOBA_STAGE
    assert nb % (4 * G) == 0
    npair = MOBA_W // LANES
    items = [(i, st) for st in range(nb // G) for i in range(G * st + G, nb)]
    assert len(items) % 4 == 0
    tile_tab = jnp.asarray([i for i, _ in items], jnp.int32)
    stage_tab = jnp.asarray([st for _, st in items], jnp.int32)
    qkspec = pl.BlockSpec((seq, LANES), lambda b, hp, *_: (b, hp))
    vtspec = pl.BlockSpec((1, 1, nb, LANES, BLK), lambda b, hp, *_: (b, hp, 0, 0, 0))
    stat = pltpu.VMEM((nb, 1, 2 * BLK), F32)
    sbuf = pltpu.VMEM((G * BLK, 2 * BLK), F32)
    pbuf = pltpu.VMEM((G * BLK, 2 * BLK), BF16)
    return pl.pallas_call(
        functools.partial(_moba_kernel, n_items=len(items)),
        grid_spec=pltpu.PrefetchScalarGridSpec(
            num_scalar_prefetch=2,
            grid=(batch, npair),
            in_specs=[qkspec, qkspec, vtspec],
            out_specs=qkspec,
            scratch_shapes=[pltpu.VMEM((nb, LANES), F32), pltpu.VMEM((nb, nb, 2 * BLK), F32),
                            pltpu.VMEM((nb, 2 * BLK, LANES), BF16),
                            stat, stat, stat, pltpu.VMEM((nb, LANES, BLK), F32),
                            sbuf, sbuf, sbuf, sbuf, pbuf, pbuf, pbuf, pbuf]),
        out_shape=jax.ShapeDtypeStruct((batch * seq, MOBA_W), BF16),
        compiler_params=pltpu.CompilerParams(dimension_semantics=("arbitrary",) * 2,
                                             vmem_limit_bytes=VMEM_LIMIT),
        name="moba",
    )(tile_tab, stage_tab, mq, mk, mvt)


def _merge_kernel(x_ref, ret_ref, moba_ref, ga_ref, gb_ref, wr_ref, wm_ref, wo_ref, o_ref):
    a = _dot(ret_ref[...], wr_ref[...])
    b = _dot(moba_ref[...], wm_ref[...])
    mix = ga_ref[...].astype(F32) * a + gb_ref[...].astype(F32) * b
    o_ref[...] = x_ref[...] + _dot(mix.astype(BF16), wo_ref[...])


def _merge(x2, ret, moba, sga, sgb, w_ret_out, w_moba_out, w_o, *, tm):
    T = x2.shape[0]
    row = lambda w: pl.BlockSpec((tm, w), lambda i: (i, 0))
    return pl.pallas_call(
        _merge_kernel,
        grid=(T // tm,),
        in_specs=[row(D_MODEL), row(RET_V), row(MOBA_W), row(D_MODEL), row(D_MODEL),
                  _const_spec((RET_V, D_MODEL)), _const_spec((MOBA_W, D_MODEL)), _const_spec((D_MODEL, D_MODEL))],
        out_specs=row(D_MODEL),
        out_shape=jax.ShapeDtypeStruct((T, D_MODEL), F32),
        compiler_params=pltpu.CompilerParams(dimension_semantics=("arbitrary",), vmem_limit_bytes=VMEM_LIMIT),
        name="merge",
    )(x2, ret, moba, sga, sgb, w_ret_out.astype(BF16), w_moba_out.astype(BF16), w_o.astype(BF16))


def _ffn_kernel(x_ref, n2_ref, wg_ref, wu_ref, wd_ref, o_ref):
    x = x_ref[...]
    ms = jnp.mean(x * x, axis=-1, keepdims=True)
    h = (x * lax.rsqrt(ms + RMS_EPS) * n2_ref[...]).astype(BF16)
    g = _dot(h, wg_ref[...])
    u = _dot(h, wu_ref[...])
    act = (g * jax.nn.sigmoid(g) * u).astype(BF16)
    o_ref[...] = x + _dot(act, wd_ref[...])


def _ffn(x2, norm2_w, w_gate, w_up, w_down, *, tm):
    T = x2.shape[0]
    row = pl.BlockSpec((tm, D_MODEL), lambda i: (i, 0))
    return pl.pallas_call(
        _ffn_kernel,
        grid=(T // tm,),
        in_specs=[row, _const_spec((1, D_MODEL)), _const_spec((D_MODEL, FFN_HIDDEN)),
                  _const_spec((D_MODEL, FFN_HIDDEN)), _const_spec((FFN_HIDDEN, D_MODEL))],
        out_specs=row,
        out_shape=jax.ShapeDtypeStruct((T, D_MODEL), F32),
        compiler_params=pltpu.CompilerParams(dimension_semantics=("arbitrary",), vmem_limit_bytes=VMEM_LIMIT),
        name="ffn",
    )(x2, norm2_w.astype(F32)[None, :], w_gate.astype(BF16), w_up.astype(BF16), w_down.astype(BF16))


def kernel(x, norm1_w, w_in, q_norm_w, k_norm_w, w_ret_out, w_moba_out, w_o, norm2_w, w_ffn_gate, w_ffn_up, w_ffn_down):
    B, S, D = x.shape
    assert D == D_MODEL and S % MOBA_BLOCK == 0 and S % MERGE_ROW_TILE == 0 and S % ROW_TILE == 0
    depth = norm1_w.shape[0]
    x2 = x.reshape(B * S, D)
    for l in range(depth):
        rq, rk, rv, rgs, mq, mk, mvt, sga, sgb = _inproj(
            x2, norm1_w[l], w_in[l], q_norm_w[l], k_norm_w[l], seq=S, tm=ROW_TILE)
        ret = _retention(rq, rk, rv, rgs, batch=B, seq=S, rt=ROW_TILE)
        moba = _moba(mq, mk, mvt, batch=B, seq=S)
        x2 = _merge(x2, ret, moba, sga, sgb, w_ret_out[l], w_moba_out[l], w_o[l], tm=MERGE_ROW_TILE)
        x2 = _ffn(x2, norm2_w[l], w_ffn_gate[l], w_ffn_up[l], w_ffn_down[l], tm=ROW_TILE)
    return x2.reshape(B, S, D)
```

```python
import functools
import math

import numpy as np
import jax
import jax.numpy as jnp
from jax import lax
from jax.experimental import pallas as pl
from jax.experimental.pallas import tpu as pltpu

D_MODEL = 1024
RET_HEADS = 4
RET_DK = 128
RET_DV = 256
ROPE_BASE = 10000.0
MOBA_HEADS = 8
MOBA_DH = 64
MOBA_BLOCK = 256
MOBA_TOPK = 3
FFN_HIDDEN = 2816
RET_QK = RET_HEADS * RET_DK
RET_V = RET_HEADS * RET_DV
MOBA_W = MOBA_HEADS * MOBA_DH
IN_SPLITS = (RET_QK, RET_QK, RET_V, RET_V, MOBA_W, MOBA_W, MOBA_W, D_MODEL, D_MODEL)
IN_COLS = sum(IN_SPLITS)
RMS_EPS = 1e-6
GN_EPS = 1e-5
NEG = -1e30

LANES = 128
BF16_ROWS = 16
VMEM_LIMIT = 52 * 1024 * 1024
RET_CHUNK = 256
MOBA_STAGE = 4
ROW_TILE = 512
MERGE_ROW_TILE = 1024
FFN_ROW_TILE = 1024
FFN_COL_CHUNKS = ((0, 1536), (1536, FFN_HIDDEN))

F32 = jnp.float32
BF16 = jnp.bfloat16


def _dot(a, b):
    return jnp.dot(a, b, preferred_element_type=F32)


def _dot_nt(a, b):
    return lax.dot_general(a, b, (((1,), (1,)), ((), ())), preferred_element_type=F32)


def _dot_tn(a, b):
    return lax.dot_general(a, b, (((0,), (0,)), ((), ())), preferred_element_type=F32)


def _const_spec(shape):
    nd = len(shape)
    return pl.BlockSpec(shape, lambda *_: (0,) * nd, pipeline_mode=pl.Buffered(1))


def _inproj_kernel(x_ref, xn_ref, n1_ref, w_ref, qn_ref, kn_ref, cos_ref, sin_ref,
                   rq_ref, rk_ref, rv_ref, rg_ref, mq_ref, mk_ref, mvt_ref, ga_ref, gb_ref, h_ref,
                   *, tm):
    i = pl.program_id(0)

    def normed(x):
        ms = jnp.mean(x * x, axis=-1, keepdims=True)
        return (x * lax.rsqrt(ms + RMS_EPS) * n1_ref[...]).astype(BF16)

    @pl.when(i == 0)
    def _():
        h_ref[0] = normed(x_ref[...])

    h = h_ref[i % 2]

    offs = np.cumsum((0,) + IN_SPLITS)

    def proj(n):
        return _dot(h, w_ref[:, int(offs[n]):int(offs[n + 1])])

    cos = cos_ref[...]
    sin = sin_ref[...]
    lane = lax.broadcasted_iota(jnp.int32, (tm, LANES), 1)

    def rotary_store(p, out_ref, scale):
        for hd in range(RET_HEADS):
            xh = p[:, hd * RET_DK:(hd + 1) * RET_DK]
            y = xh * cos + pltpu.roll(xh, RET_DK // 2, 1) * sin
            if scale is not None:
                y = y * scale
            out_ref[:, hd * RET_DK:(hd + 1) * RET_DK] = y.astype(BF16)

    rotary_store(proj(0), rq_ref, None)
    rotary_store(proj(1), rk_ref, RET_DK ** -0.5)
    rv_ref[...] = proj(2).astype(BF16)
    rg = proj(3)
    rg_ref[...] = (rg * jax.nn.sigmoid(rg)).astype(BF16)

    low_half = lane < MOBA_DH

    def head_rms_store(p, w_row_ref, out_ref, scale):
        for g in range(MOBA_W // LANES):
            cols = slice(g * LANES, (g + 1) * LANES)
            ph = p[:, cols]
            sq = ph * ph
            lo = jnp.sum(jnp.where(low_half, sq, 0.0), axis=-1, keepdims=True)
            hi = jnp.sum(jnp.where(low_half, 0.0, sq), axis=-1, keepdims=True)
            msq = jnp.where(low_half, lo, hi) * (1.0 / MOBA_DH)
            y = ph * lax.rsqrt(msq + RMS_EPS) * w_row_ref[:, cols]
            out_ref[:, cols] = (y if scale is None else y * scale).astype(BF16)

    head_rms_store(proj(4), qn_ref, mq_ref, MOBA_DH ** -0.5 * math.log2(math.e))
    head_rms_store(proj(5), kn_ref, mk_ref, None)
    mvt = proj(6).T.astype(BF16)
    for pair in range(MOBA_W // LANES):
        for c in range(tm // MOBA_BLOCK):
            mvt_ref[0, pair, c] = mvt[pair * LANES:(pair + 1) * LANES, c * MOBA_BLOCK:(c + 1) * MOBA_BLOCK]
    ga_ref[...] = jax.nn.sigmoid(proj(7)).astype(BF16)
    gb_ref[...] = jax.nn.sigmoid(proj(8)).astype(BF16)
    h_ref[(i + 1) % 2] = normed(xn_ref[...])


def _inproj(x2, norm1_w, w_in, q_norm_w, k_norm_w, *, seq, tm):
    T = x2.shape[0]
    half = RET_DK // 2
    inv = ROPE_BASE ** (-np.arange(half, dtype=np.float64) / half)
    ang = np.arange(seq, dtype=np.float64)[:, None] * inv[None, :]
    cos2 = jnp.asarray(np.concatenate([np.cos(ang), np.cos(ang)], axis=1), F32)
    sin2 = jnp.asarray(np.concatenate([-np.sin(ang), np.sin(ang)], axis=1), F32)
    qn = jnp.tile(q_norm_w.astype(F32), MOBA_HEADS)[None, :]
    kn = jnp.tile(k_norm_w.astype(F32), MOBA_HEADS)[None, :]
    offs = np.cumsum((0,) + IN_SPLITS)
    n_tiles = T // tm
    nt = seq // tm
    npair = MOBA_W // LANES
    nb = seq // MOBA_BLOCK
    row = lambda w: pl.BlockSpec((tm, w), lambda i: (i, 0))
    next_row = pl.BlockSpec((tm, D_MODEL), lambda i: (jnp.minimum(i + 1, n_tiles - 1), 0))
    pos_row = pl.BlockSpec((tm, LANES), lambda i: (i % nt, 0))
    row_out = lambda w: (row(w), jax.ShapeDtypeStruct((T, w), BF16))
    mvt_out = (pl.BlockSpec((1, npair, tm // MOBA_BLOCK, LANES, MOBA_BLOCK), lambda i: (i // nt, 0, i % nt, 0, 0)),
               jax.ShapeDtypeStruct((T // seq, npair, nb, LANES, MOBA_BLOCK), BF16))
    outs = [row_out(RET_QK), row_out(RET_QK), row_out(RET_V), row_out(RET_V), row_out(MOBA_W), row_out(MOBA_W),
            mvt_out, row_out(D_MODEL), row_out(D_MODEL)]
    return pl.pallas_call(
        functools.partial(_inproj_kernel, tm=tm),
        grid=(n_tiles,),
        in_specs=[row(D_MODEL), next_row, _const_spec((1, D_MODEL)), _const_spec((D_MODEL, IN_COLS)),
                  _const_spec((1, MOBA_W)), _const_spec((1, MOBA_W)), pos_row, pos_row],
        out_specs=[o[0] for o in outs],
        out_shape=[o[1] for o in outs],
        scratch_shapes=[pltpu.VMEM((2, tm, D_MODEL), BF16)],
        compiler_params=pltpu.CompilerParams(dimension_semantics=("arbitrary",), vmem_limit_bytes=VMEM_LIMIT),
        name="inproj",
    )(x2, x2, norm1_w.astype(F32)[None, :], w_in.astype(BF16), qn, kn, cos2, sin2)


def _retention_kernel(q_ref, k_ref, v_ref, g_ref, dmat_ref, xi_ref, zeta_ref, o_ref, state_ref,
                      *, rt, chunk, g_chunk):
    @pl.when(pl.program_id(1) == 0)
    def _():
        state_ref[...] = jnp.zeros_like(state_ref)

    for c in range(rt // chunk):
        rows = slice(c * chunk, (c + 1) * chunk)
        for hd in range(RET_HEADS):
            kcols = slice(hd * RET_DK, (hd + 1) * RET_DK)
            vcols = slice(hd * RET_DV, (hd + 1) * RET_DV)
            q = q_ref[rows, kcols]
            k = k_ref[rows, kcols]
            v = v_ref[rows, vcols]
            state = state_ref[hd]
            scores = _dot_nt(q, k) * dmat_ref[hd]
            q_dec = (q.astype(F32) * xi_ref[hd]).astype(BF16)
            o = _dot(scores.astype(BF16), v) + _dot(q_dec, state.astype(BF16))
            k_dec = (k.astype(F32) * zeta_ref[hd]).astype(BF16)
            state_ref[hd] = g_chunk[hd] * state + _dot_tn(k_dec, v)
            mu = jnp.mean(o, axis=-1, keepdims=True)
            d = o - mu
            var = jnp.mean(d * d, axis=-1, keepdims=True)
            y = d * lax.rsqrt(var + GN_EPS)
            o_ref[rows, vcols] = (g_ref[rows, vcols].astype(F32) * y).astype(BF16)


def _retention(rq, rk, rv, rgs, *, batch, seq, rt):
    C = RET_CHUNK
    lg = np.log1p(-np.exp2(-5.0 - np.arange(RET_HEADS, dtype=np.float64)))
    idx = np.arange(C, dtype=np.float64)
    diff = idx[:, None] - idx[None, :]
    dmat = np.where(diff >= 0, np.exp(np.maximum(diff, 0.0)[None] * lg[:, None, None]), 0.0)
    xi = np.broadcast_to(np.exp((idx + 1.0)[None, :] * lg[:, None])[:, :, None], (RET_HEADS, C, LANES))
    zeta = np.broadcast_to(np.exp((C - 1.0 - idx)[None, :] * lg[:, None])[:, :, None], (RET_HEADS, C, LANES))
    g_chunk = tuple(float(v) for v in np.exp(C * lg))
    nt = seq // rt
    row = lambda w: pl.BlockSpec((rt, w), lambda b, j: (b * nt + j, 0))
    return pl.pallas_call(
        functools.partial(_retention_kernel, rt=rt, chunk=C, g_chunk=g_chunk),
        grid=(batch, nt),
        in_specs=[row(RET_QK), row(RET_QK), row(RET_V), row(RET_V),
                  _const_spec((RET_HEADS, C, C)), _const_spec((RET_HEADS, C, LANES)),
                  _const_spec((RET_HEADS, C, LANES))],
        out_specs=row(RET_V),
        out_shape=jax.ShapeDtypeStruct((batch * seq, RET_V), BF16),
        scratch_shapes=[pltpu.VMEM((RET_HEADS, RET_DK, RET_DV), F32)],
        compiler_params=pltpu.CompilerParams(dimension_semantics=("arbitrary", "arbitrary"),
                                             vmem_limit_bytes=VMEM_LIMIT),
        name="retention",
    )(rq, rk, rv, rgs, jnp.asarray(dmat, F32), jnp.asarray(xi, F32), jnp.asarray(zeta, F32))


def _moba_kernel(tile_tab_ref, stage_tab_ref, q_ref, k_ref, vt_ref, o_ref,
                 kbar_ref, bias_ref, qcat_ref, m_ref, r_ref, l_ref, acc_ref,
                 s0_ref, s1_ref, s2_ref, s3_ref, p0_ref, p1_ref, p2_ref, p3_ref, *, n_items):
    BLK = MOBA_BLOCK
    DH = MOBA_DH
    G = MOBA_STAGE
    STG = G * BLK
    nb = vt_ref.shape[2]
    nbp = kbar_ref.shape[0]
    s_bufs = (s0_ref, s1_ref, s2_ref, s3_ref)
    p_bufs = (p0_ref, p1_ref, p2_ref, p3_ref)
    lane = lax.broadcasted_iota(jnp.int32, (1, LANES), 1)

    def q_cat_body(i, c):
        q_pair = q_ref[pl.ds(pl.multiple_of(i * BLK, BLK), BLK), :]
        zero = jnp.zeros_like(q_pair)
        qcat_ref[i, :BLK, :] = jnp.where(lane < DH, q_pair, zero)
        qcat_ref[i, BLK:, :] = jnp.where(lane >= DH, q_pair, zero)
        return c

    lax.fori_loop(0, nb, q_cat_body, 0)

    def q_cat_of(i):
        return qcat_ref[i]

    kbar_ref[...] = jnp.zeros_like(kbar_ref)

    def kbar_body(n, c):
        kb = k_ref[pl.ds(pl.multiple_of(n * BLK, BLK), BLK), :]
        kbar_ref[pl.ds(n, 1), :] = jnp.mean(kb.astype(F32), axis=0, keepdims=True)
        return c

    lax.fori_loop(0, nb, kbar_body, 0)

    kbar = kbar_ref[...]
    kbar_hi = kbar.astype(BF16)
    kbar_lo = (kbar - kbar_hi.astype(F32)).astype(BF16)
    blk_idx = lax.broadcasted_iota(jnp.int32, (nbp, 2 * BLK), 0)

    def select_blocks(i):
        q_cat = q_cat_of(i)
        gate = _dot_nt(kbar_hi, q_cat) + _dot_nt(kbar_lo, q_cat)
        past = blk_idx < i
        gate = jnp.where(past, gate, NEG)
        sel = jnp.zeros((nbp, 2 * BLK), jnp.bool_)
        for _ in range(MOBA_TOPK):
            best = jnp.max(gate, axis=0, keepdims=True)
            first = jnp.min(jnp.where(gate == best, blk_idx, nbp), axis=0, keepdims=True)
            pick = blk_idx == first
            sel = sel | (pick & past)
            gate = jnp.where(pick, -3e38, gate)
        bias_ref[i] = jnp.where(sel, 0.0, NEG)

    def select_body(u, c):
        select_blocks(2 * u)
        select_blocks(2 * u + 1)
        return c

    lax.fori_loop(0, nb // 2, select_body, 0)

    m_ref[...] = jnp.full(m_ref.shape, NEG, F32)
    r_ref[...] = jnp.full(r_ref.shape, NEG, F32)
    l_ref[...] = jnp.zeros_like(l_ref)
    acc_ref[...] = jnp.zeros_like(acc_ref)

    def scores(i, st, dst_ref):
        ks = k_ref[pl.ds(pl.multiple_of(st * STG, STG), STG), :]
        dst_ref[...] = _dot_nt(ks, q_cat_of(i))

    def values(st, p_ref):
        vt = jnp.concatenate([vt_ref[0, 0, G * st + g] for g in range(G)], axis=1)
        pv = _dot(jnp.concatenate([vt, jnp.ones((BF16_ROWS, STG), BF16)], axis=0), p_ref[...])
        return pv[:DH, :BLK], pv[DH:2 * DH, BLK:], pv[2 * DH:2 * DH + 1, :]

    def fold(pv, m_pv, i):
        num0, num1, sums = pv
        scale = jnp.exp2(r_ref[i] - m_pv)
        r_ref[i] = m_pv
        l_ref[i] = scale * l_ref[i] + sums
        acc_ref[i, :DH, :] = scale[:, :BLK] * acc_ref[i, :DH, :] + num0
        acc_ref[i, DH:, :] = scale[:, BLK:] * acc_ref[i, DH:, :] + num1

    key_idx = lax.broadcasted_iota(jnp.int32, (BLK, 2 * BLK), 0)
    qry_idx = lax.broadcasted_iota(jnp.int32, (BLK, 2 * BLK), 1) & (BLK - 1)
    causal = key_idx <= qry_idx

    def make_probs(own):
        def probs(i, st, s_ref, p_ref, tok):
            n_live = G if own is None else own + 1
            tiles, shifts = [], []
            m_new = m_ref[i]
            for g in range(n_live):
                sg = s_ref[g * BLK:(g + 1) * BLK, :]
                if g == own:
                    sg = jnp.where(causal, sg, NEG)
                    bg = None
                    m_new = jnp.maximum(m_new, jnp.max(sg, axis=0, keepdims=True))
                else:
                    bg = bias_ref[i, pl.ds(G * st + g, 1), :]
                    m_new = jnp.maximum(m_new, jnp.max(sg, axis=0, keepdims=True) + bg)
                tiles.append(sg)
                shifts.append(bg)
            m_ref[i] = m_new
            m_new = m_new + tok
            for g in range(G):
                rows = slice(g * BLK, (g + 1) * BLK)
                if g >= n_live:
                    p_ref[rows, :] = jnp.zeros((BLK, 2 * BLK), BF16)
                else:
                    shift = m_new if shifts[g] is None else m_new - shifts[g]
                    p_ref[rows, :] = jnp.exp2(tiles[g] - shift).astype(BF16)
            return m_new
        return probs

    def run_items(count, item, probs, tok0):
        def clamped(t):
            return item(jnp.clip(t, 0, count - 1))

        scores(*clamped(0), s_bufs[0])
        scores(*clamped(1), s_bufs[1])

        def slot(t, k, carry):
            m_prev, toks = carry
            scores(*clamped(t + 2), s_bufs[(k + 2) % 4])
            i_prev, st_prev = clamped(t - 1)
            pv = values(st_prev, p_bufs[(k - 1) % 4])
            i, st = clamped(t)
            m_t = probs(i, st, s_bufs[k % 4], p_bufs[k % 4], toks[0])
            fold(pv, m_prev, i_prev)
            return m_t, (toks[1], toks[2], pv[2] * 0.0)

        unroll = 4

        def body(it, carry):
            for k in range(unroll):
                carry = slot(unroll * it + k, k, carry)
            return carry

        m_last, _ = lax.fori_loop(0, count // unroll, body, (r_ref[clamped(0)[0]], (tok0, tok0, tok0)))
        i_last, st_last = clamped(count - 1)
        pv_last = values(st_last, p_bufs[(count - 1) % 4])
        fold(pv_last, m_last, i_last)
        tok_last = pv_last[2] * 0.0
        p_bufs[3][...] = jnp.broadcast_to(tok_last.astype(BF16), p_bufs[3].shape)
        return tok_last

    p_bufs[3][...] = jnp.zeros_like(p_bufs[3])
    tok = jnp.zeros((1, 2 * BLK), F32)
    for own in range(G):
        tok = run_items(nb // G, lambda u, own=own: (G * u + own, u), make_probs(own), tok)
    run_items(n_items, lambda t: (tile_tab_ref[t], stage_tab_ref[t]), make_probs(None), tok)

    def finish(i, c):
        l = l_ref[i]
        acc = acc_ref[i]
        out_t = jnp.concatenate([acc[:DH] / l[:, :BLK], acc[DH:] / l[:, BLK:]], axis=0)
        o_ref[pl.ds(pl.multiple_of(i * BLK, BLK), BLK), :] = out_t.T.astype(BF16)
        return c

    lax.fori_loop(0, nb, finish, 0)


def _moba(mq, mk, mvt, *, batch, seq):
    BLK = MOBA_BLOCK
    nb = seq // BLK
    G = MOBA_STAGE
    assert nb % (4 * G) == 0
    npair = MOBA_W // LANES
    items = [(i, st) for st in range(nb // G) for i in range(G * st + G, nb)]
    assert len(items) % 4 == 0
    tile_tab = jnp.asarray([i for i, _ in items], jnp.int32)
    stage_tab = jnp.asarray([st for _, st in items], jnp.int32)
    qkspec = pl.BlockSpec((seq, LANES), lambda b, hp, *_: (b, hp))
    vtspec = pl.BlockSpec((1, 1, nb, LANES, BLK), lambda b, hp, *_: (b, hp, 0, 0, 0))
    stat = pltpu.VMEM((nb, 1, 2 * BLK), F32)
    sbuf = pltpu.VMEM((G * BLK, 2 * BLK), F32)
    pbuf = pltpu.VMEM((G * BLK, 2 * BLK), BF16)
    return pl.pallas_call(
        functools.partial(_moba_kernel, n_items=len(items)),
        grid_spec=pltpu.PrefetchScalarGridSpec(
            num_scalar_prefetch=2,
            grid=(batch, npair),
            in_specs=[qkspec, qkspec, vtspec],
            out_specs=qkspec,
            scratch_shapes=[pltpu.VMEM((nb, LANES), F32), pltpu.VMEM((nb, nb, 2 * BLK), F32),
                            pltpu.VMEM((nb, 2 * BLK, LANES), BF16),
                            stat, stat, stat, pltpu.VMEM((nb, LANES, BLK), F32),
                            sbuf, sbuf, sbuf, sbuf, pbuf, pbuf, pbuf, pbuf]),
        out_shape=jax.ShapeDtypeStruct((batch * seq, MOBA_W), BF16),
        compiler_params=pltpu.CompilerParams(dimension_semantics=("arbitrary",) * 2,
                                             vmem_limit_bytes=VMEM_LIMIT),
        name="moba",
    )(tile_tab, stage_tab, mq, mk, mvt)


def _merge_kernel(x_ref, ret_ref, moba_ref, ga_ref, gb_ref, wr_ref, wm_ref, wo_ref, o_ref):
    a = _dot(ret_ref[...], wr_ref[...])
    b = _dot(moba_ref[...], wm_ref[...])
    mix = ga_ref[...].astype(F32) * a + gb_ref[...].astype(F32) * b
    o_ref[...] = x_ref[...] + _dot(mix.astype(BF16), wo_ref[...])


def _merge(x2, ret, moba, sga, sgb, w_ret_out, w_moba_out, w_o, *, tm):
    T = x2.shape[0]
    row = lambda w: pl.BlockSpec((tm, w), lambda i: (i, 0))
    return pl.pallas_call(
        _merge_kernel,
        grid=(T // tm,),
        in_specs=[row(D_MODEL), row(RET_V), row(MOBA_W), row(D_MODEL), row(D_MODEL),
                  _const_spec((RET_V, D_MODEL)), _const_spec((MOBA_W, D_MODEL)), _const_spec((D_MODEL, D_MODEL))],
        out_specs=row(D_MODEL),
        out_shape=jax.ShapeDtypeStruct((T, D_MODEL), F32),
        compiler_params=pltpu.CompilerParams(dimension_semantics=("arbitrary",), vmem_limit_bytes=VMEM_LIMIT),
        name="merge",
    )(x2, ret, moba, sga, sgb, w_ret_out.astype(BF16), w_moba_out.astype(BF16), w_o.astype(BF16))


def _ffn_kernel(x_ref, n2_ref, wg_ref, wu_ref, wd_ref, o_ref):
    x = x_ref[...]
    ms = jnp.mean(x * x, axis=-1, keepdims=True)
    h = (x * lax.rsqrt(ms + RMS_EPS) * n2_ref[...]).astype(BF16)
    out = x
    for c0, c1 in FFN_COL_CHUNKS:
        g = _dot(h, wg_ref[:, c0:c1])
        u = _dot(h, wu_ref[:, c0:c1])
        act = (g * jax.nn.sigmoid(g) * u).astype(BF16)
        out = out + _dot(act, wd_ref[c0:c1, :])
    o_ref[...] = out


def _ffn(x2, norm2_w, w_gate, w_up, w_down, *, tm):
    T = x2.shape[0]
    row = pl.BlockSpec((tm, D_MODEL), lambda i: (i, 0))
    return pl.pallas_call(
        _ffn_kernel,
        grid=(T // tm,),
        in_specs=[row, _const_spec((1, D_MODEL)), _const_spec((D_MODEL, FFN_HIDDEN)),
                  _const_spec((D_MODEL, FFN_HIDDEN)), _const_spec((FFN_HIDDEN, D_MODEL))],
        out_specs=row,
        out_shape=jax.ShapeDtypeStruct((T, D_MODEL), F32),
        compiler_params=pltpu.CompilerParams(dimension_semantics=("arbitrary",), vmem_limit_bytes=VMEM_LIMIT),
        name="ffn",
    )(x2, norm2_w.astype(F32)[None, :], w_gate.astype(BF16), w_up.astype(BF16), w_down.astype(BF16))


def kernel(x, norm1_w, w_in, q_norm_w, k_norm_w, w_ret_out, w_moba_out, w_o, norm2_w, w_ffn_gate, w_ffn_up, w_ffn_down):
    B, S, D = x.shape
    assert D == D_MODEL and S % MOBA_BLOCK == 0 and S % MERGE_ROW_TILE == 0 and S % ROW_TILE == 0
    depth = norm1_w.shape[0]
    x2 = x.reshape(B * S, D)
    for l in range(depth):
        rq, rk, rv, rgs, mq, mk, mvt, sga, sgb = _inproj(
            x2, norm1_w[l], w_in[l], q_norm_w[l], k_norm_w[l], seq=S, tm=ROW_TILE)
        ret = _retention(rq, rk, rv, rgs, batch=B, seq=S, rt=ROW_TILE)
        moba = _moba(mq, mk, mvt, batch=B, seq=S)
        x2 = _merge(x2, ret, moba, sga, sgb, w_ret_out[l], w_moba_out[l], w_o[l], tm=MERGE_ROW_TILE)
        x2 = _ffn(x2, norm2_w[l], w_ffn_gate[l], w_ffn_up[l], w_ffn_down[l], tm=FFN_ROW_TILE)
    return x2.reshape(B, S, D)
```

```python
import functools
import math

import numpy as np
import jax
import jax.numpy as jnp
from jax import lax
from jax.experimental import pallas as pl
from jax.experimental.pallas import tpu as pltpu

D_MODEL = 1024
RET_HEADS = 4
RET_DK = 128
RET_DV = 256
ROPE_BASE = 10000.0
MOBA_HEADS = 8
MOBA_DH = 64
MOBA_BLOCK = 256
MOBA_TOPK = 3
FFN_HIDDEN = 2816
RET_QK = RET_HEADS * RET_DK
RET_V = RET_HEADS * RET_DV
MOBA_W = MOBA_HEADS * MOBA_DH
IN_SPLITS = (RET_QK, RET_QK, RET_V, RET_V, MOBA_W, MOBA_W, MOBA_W, D_MODEL, D_MODEL)
IN_COLS = sum(IN_SPLITS)
RMS_EPS = 1e-6
GN_EPS = 1e-5
NEG = -1e30

LANES = 128
BF16_ROWS = 16
VMEM_LIMIT = 52 * 1024 * 1024
RET_CHUNK = 256
MOBA_STAGE = 4
ROW_TILE = 512
MERGE_ROW_TILE = 1024
FFN_ROW_TILE = 1024
FFN_COL_CHUNKS = ((0, 1536), (1536, FFN_HIDDEN))

F32 = jnp.float32
BF16 = jnp.bfloat16


def _dot(a, b):
    return jnp.dot(a, b, preferred_element_type=F32)


def _dot_nt(a, b):
    return lax.dot_general(a, b, (((1,), (1,)), ((), ())), preferred_element_type=F32)


def _dot_tn(a, b):
    return lax.dot_general(a, b, (((0,), (0,)), ((), ())), preferred_element_type=F32)


def _const_spec(shape):
    nd = len(shape)
    return pl.BlockSpec(shape, lambda *_: (0,) * nd, pipeline_mode=pl.Buffered(1))


def _inproj_kernel(x_ref, xn_ref, n1_ref, w_ref, qn_ref, kn_ref, cos_ref, sin_ref,
                   rq_ref, rk_ref, rv_ref, rg_ref, mq_ref, mk_ref, mvt_ref, ga_ref, gb_ref, h_ref,
                   *, tm):
    i = pl.program_id(0)

    def normed(x):
        ms = jnp.mean(x * x, axis=-1, keepdims=True)
        return (x * lax.rsqrt(ms + RMS_EPS) * n1_ref[...]).astype(BF16)

    @pl.when(i == 0)
    def _():
        h_ref[0] = normed(x_ref[...])

    h = h_ref[i % 2]

    offs = np.cumsum((0,) + IN_SPLITS)

    def proj(n):
        return _dot(h, w_ref[:, int(offs[n]):int(offs[n + 1])])

    cos = cos_ref[...]
    sin = sin_ref[...]
    lane = lax.broadcasted_iota(jnp.int32, (tm, LANES), 1)

    def rotary_store(p, out_ref, scale):
        for hd in range(RET_HEADS):
            xh = p[:, hd * RET_DK:(hd + 1) * RET_DK]
            y = xh * cos + pltpu.roll(xh, RET_DK // 2, 1) * sin
            if scale is not None:
                y = y * scale
            out_ref[:, hd * RET_DK:(hd + 1) * RET_DK] = y.astype(BF16)

    rotary_store(proj(0), rq_ref, None)
    rotary_store(proj(1), rk_ref, RET_DK ** -0.5)
    rv_ref[...] = proj(2).astype(BF16)
    rg = proj(3)
    rg_ref[...] = (rg * jax.nn.sigmoid(rg)).astype(BF16)

    low_half = lane < MOBA_DH

    def head_rms_store(p, w_row_ref, out_ref, scale):
        for g in range(MOBA_W // LANES):
            cols = slice(g * LANES, (g + 1) * LANES)
            ph = p[:, cols]
            sq = ph * ph
            lo = jnp.sum(jnp.where(low_half, sq, 0.0), axis=-1, keepdims=True)
            hi = jnp.sum(jnp.where(low_half, 0.0, sq), axis=-1, keepdims=True)
            msq = jnp.where(low_half, lo, hi) * (1.0 / MOBA_DH)
            y = ph * lax.rsqrt(msq + RMS_EPS) * w_row_ref[:, cols]
            out_ref[:, cols] = (y if scale is None else y * scale).astype(BF16)

    head_rms_store(proj(4), qn_ref, mq_ref, MOBA_DH ** -0.5 * math.log2(math.e))
    head_rms_store(proj(5), kn_ref, mk_ref, None)
    mvt = proj(6).T.astype(BF16)
    for pair in range(MOBA_W // LANES):
        for c in range(tm // MOBA_BLOCK):
            mvt_ref[0, pair, c] = mvt[pair * LANES:(pair + 1) * LANES, c * MOBA_BLOCK:(c + 1) * MOBA_BLOCK]
    ga_ref[...] = jax.nn.sigmoid(proj(7)).astype(BF16)
    gb_ref[...] = jax.nn.sigmoid(proj(8)).astype(BF16)
    h_ref[(i + 1) % 2] = normed(xn_ref[...])


def _inproj(x2, norm1_w, w_in, q_norm_w, k_norm_w, *, seq, tm):
    T = x2.shape[0]
    half = RET_DK // 2
    inv = ROPE_BASE ** (-np.arange(half, dtype=np.float64) / half)
    ang = np.arange(seq, dtype=np.float64)[:, None] * inv[None, :]
    cos2 = jnp.asarray(np.concatenate([np.cos(ang), np.cos(ang)], axis=1), F32)
    sin2 = jnp.asarray(np.concatenate([-np.sin(ang), np.sin(ang)], axis=1), F32)
    qn = jnp.tile(q_norm_w.astype(F32), MOBA_HEADS)[None, :]
    kn = jnp.tile(k_norm_w.astype(F32), MOBA_HEADS)[None, :]
    offs = np.cumsum((0,) + IN_SPLITS)
    n_tiles = T // tm
    nt = seq // tm
    npair = MOBA_W // LANES
    nb = seq // MOBA_BLOCK
    row = lambda w: pl.BlockSpec((tm, w), lambda i: (i, 0))
    next_row = pl.BlockSpec((tm, D_MODEL), lambda i: (jnp.minimum(i + 1, n_tiles - 1), 0))
    pos_row = pl.BlockSpec((tm, LANES), lambda i: (i % nt, 0))
    row_out = lambda w: (row(w), jax.ShapeDtypeStruct((T, w), BF16))
    mvt_out = (pl.BlockSpec((1, npair, tm // MOBA_BLOCK, LANES, MOBA_BLOCK), lambda i: (i // nt, 0, i % nt, 0, 0)),
               jax.ShapeDtypeStruct((T // seq, npair, nb, LANES, MOBA_BLOCK), BF16))
    outs = [row_out(RET_QK), row_out(RET_QK), row_out(RET_V), row_out(RET_V), row_out(MOBA_W), row_out(MOBA_W),
            mvt_out, row_out(D_MODEL), row_out(D_MODEL)]
    return pl.pallas_call(
        functools.partial(_inproj_kernel, tm=tm),
        grid=(n_tiles,),
        in_specs=[row(D_MODEL), next_row, _const_spec((1, D_MODEL)), _const_spec((D_MODEL, IN_COLS)),
                  _const_spec((1, MOBA_W)), _const_spec((1, MOBA_W)), pos_row, pos_row],
        out_specs=[o[0] for o in outs],
        out_shape=[o[1] for o in outs],
        scratch_shapes=[pltpu.VMEM((2, tm, D_MODEL), BF16)],
        compiler_params=pltpu.CompilerParams(dimension_semantics=("arbitrary",), vmem_limit_bytes=VMEM_LIMIT),
        name="inproj",
    )(x2, x2, norm1_w.astype(F32)[None, :], w_in.astype(BF16), qn, kn, cos2, sin2)


def _retention_kernel(q_ref, k_ref, v_ref, g_ref, dmat_ref, xi_ref, zeta_ref, o_ref, state_ref,
                      *, rt, chunk, g_chunk):
    @pl.when(pl.program_id(1) == 0)
    def _():
        state_ref[...] = jnp.zeros_like(state_ref)

    for c in range(rt // chunk):
        rows = slice(c * chunk, (c + 1) * chunk)
        for hd in range(RET_HEADS):
            kcols = slice(hd * RET_DK, (hd + 1) * RET_DK)
            vcols = slice(hd * RET_DV, (hd + 1) * RET_DV)
            q = q_ref[rows, kcols]
            k = k_ref[rows, kcols]
            v = v_ref[rows, vcols]
            state = state_ref[hd]
            scores = _dot_nt(q, k) * dmat_ref[hd]
            q_dec = (q.astype(F32) * xi_ref[hd]).astype(BF16)
            o = _dot(scores.astype(BF16), v) + _dot(q_dec, state.astype(BF16))
            k_dec = (k.astype(F32) * zeta_ref[hd]).astype(BF16)
            state_ref[hd] = g_chunk[hd] * state + _dot_tn(k_dec, v)
            mu = jnp.mean(o, axis=-1, keepdims=True)
            d = o - mu
            var = jnp.mean(d * d, axis=-1, keepdims=True)
            y = d * lax.rsqrt(var + GN_EPS)
            o_ref[rows, vcols] = (g_ref[rows, vcols].astype(F32) * y).astype(BF16)


def _retention(rq, rk, rv, rgs, *, batch, seq, rt):
    C = RET_CHUNK
    lg = np.log1p(-np.exp2(-5.0 - np.arange(RET_HEADS, dtype=np.float64)))
    idx = np.arange(C, dtype=np.float64)
    diff = idx[:, None] - idx[None, :]
    dmat = np.where(diff >= 0, np.exp(np.maximum(diff, 0.0)[None] * lg[:, None, None]), 0.0)
    xi = np.broadcast_to(np.exp((idx + 1.0)[None, :] * lg[:, None])[:, :, None], (RET_HEADS, C, LANES))
    zeta = np.broadcast_to(np.exp((C - 1.0 - idx)[None, :] * lg[:, None])[:, :, None], (RET_HEADS, C, LANES))
    g_chunk = tuple(float(v) for v in np.exp(C * lg))
    nt = seq // rt
    row = lambda w: pl.BlockSpec((rt, w), lambda b, j: (b * nt + j, 0))
    return pl.pallas_call(
        functools.partial(_retention_kernel, rt=rt, chunk=C, g_chunk=g_chunk),
        grid=(batch, nt),
        in_specs=[row(RET_QK), row(RET_QK), row(RET_V), row(RET_V),
                  _const_spec((RET_HEADS, C, C)), _const_spec((RET_HEADS, C, LANES)),
                  _const_spec((RET_HEADS, C, LANES))],
        out_specs=row(RET_V),
        out_shape=jax.ShapeDtypeStruct((batch * seq, RET_V), BF16),
        scratch_shapes=[pltpu.VMEM((RET_HEADS, RET_DK, RET_DV), F32)],
        compiler_params=pltpu.CompilerParams(dimension_semantics=("arbitrary", "arbitrary"),
                                             vmem_limit_bytes=VMEM_LIMIT),
        name="retention",
    )(rq, rk, rv, rgs, jnp.asarray(dmat, F32), jnp.asarray(xi, F32), jnp.asarray(zeta, F32))


def _moba_kernel(tile_tab_ref, stage_tab_ref, q_ref, k_ref, vt_ref, o_ref,
                 kbar_ref, bias_ref, qcat_ref, m_ref, r_ref, l_ref, acc_ref,
                 s0_ref, s1_ref, s2_ref, s3_ref, p0_ref, p1_ref, p2_ref, p3_ref, *, n_items):
    BLK = MOBA_BLOCK
    DH = MOBA_DH
    G = MOBA_STAGE
    STG = G * BLK
    nb = vt_ref.shape[2]
    nbp = kbar_ref.shape[0]
    s_bufs = (s0_ref, s1_ref, s2_ref, s3_ref)
    p_bufs = (p0_ref, p1_ref, p2_ref, p3_ref)
    lane = lax.broadcasted_iota(jnp.int32, (1, LANES), 1)

    def q_cat_body(i, c):
        q_pair = q_ref[pl.ds(pl.multiple_of(i * BLK, BLK), BLK), :]
        zero = jnp.zeros_like(q_pair)
        qcat_ref[i, :BLK, :] = jnp.where(lane < DH, q_pair, zero)
        qcat_ref[i, BLK:, :] = jnp.where(lane >= DH, q_pair, zero)
        return c

    lax.fori_loop(0, nb, q_cat_body, 0)

    def q_cat_of(i):
        return qcat_ref[i]

    kbar_ref[...] = jnp.zeros_like(kbar_ref)

    def kbar_body(n, c):
        kb = k_ref[pl.ds(pl.multiple_of(n * BLK, BLK), BLK), :]
        kbar_ref[pl.ds(n, 1), :] = jnp.mean(kb.astype(F32), axis=0, keepdims=True)
        return c

    lax.fori_loop(0, nb, kbar_body, 0)

    kbar = kbar_ref[...]
    kbar_hi = kbar.astype(BF16)
    kbar_lo = (kbar - kbar_hi.astype(F32)).astype(BF16)
    blk_idx = lax.broadcasted_iota(jnp.int32, (nbp, 2 * BLK), 0)

    def select_blocks(i):
        q_cat = q_cat_of(i)
        gate = _dot_nt(kbar_hi, q_cat) + _dot_nt(kbar_lo, q_cat)
        past = blk_idx < i
        gate = jnp.where(past, gate, NEG)
        sel = jnp.zeros((nbp, 2 * BLK), jnp.bool_)
        for _ in range(MOBA_TOPK):
            best = jnp.max(gate, axis=0, keepdims=True)
            first = jnp.min(jnp.where(gate == best, blk_idx, nbp), axis=0, keepdims=True)
            pick = blk_idx == first
            sel = sel | (pick & past)
            gate = jnp.where(pick, -3e38, gate)
        bias_ref[i] = jnp.where(sel, 0.0, NEG)

    def select_body(u, c):
        select_blocks(2 * u)
        select_blocks(2 * u + 1)
        return c

    lax.fori_loop(0, nb // 2, select_body, 0)

    m_ref[...] = jnp.full(m_ref.shape, NEG, F32)
    r_ref[...] = jnp.full(r_ref.shape, NEG, F32)
    l_ref[...] = jnp.zeros_like(l_ref)
    acc_ref[...] = jnp.zeros_like(acc_ref)

    def scores(i, st, dst_ref, n_live):
        rows = n_live * BLK
        ks = k_ref[pl.ds(pl.multiple_of(st * STG, STG), rows), :]
        dst_ref[:rows, :] = _dot_nt(ks, q_cat_of(i))

    def values(st, p_ref, n_live):
        rows = n_live * BLK
        vt = jnp.concatenate([vt_ref[0, 0, G * st + g] for g in range(n_live)], axis=1)
        pv = _dot(jnp.concatenate([vt, jnp.ones((BF16_ROWS, rows), BF16)], axis=0), p_ref[:rows, :])
        return pv[:DH, :BLK], pv[DH:2 * DH, BLK:], pv[2 * DH:2 * DH + 1, :]

    def fold(pv, m_pv, i):
        num0, num1, sums = pv
        scale = jnp.exp2(r_ref[i] - m_pv)
        r_ref[i] = m_pv
        l_ref[i] = scale * l_ref[i] + sums
        acc_ref[i, :DH, :] = scale[:, :BLK] * acc_ref[i, :DH, :] + num0
        acc_ref[i, DH:, :] = scale[:, BLK:] * acc_ref[i, DH:, :] + num1

    key_idx = lax.broadcasted_iota(jnp.int32, (BLK, 2 * BLK), 0)
    qry_idx = lax.broadcasted_iota(jnp.int32, (BLK, 2 * BLK), 1) & (BLK - 1)
    causal = key_idx <= qry_idx

    def make_probs(own):
        def probs(i, st, s_ref, p_ref, tok):
            n_live = G if own is None else own + 1
            tiles, shifts = [], []
            m_new = m_ref[i]
            for g in range(n_live):
                sg = s_ref[g * BLK:(g + 1) * BLK, :]
                if g == own:
                    sg = jnp.where(causal, sg, NEG)
                    bg = None
                    m_new = jnp.maximum(m_new, jnp.max(sg, axis=0, keepdims=True))
                else:
                    bg = bias_ref[i, pl.ds(G * st + g, 1), :]
                    m_new = jnp.maximum(m_new, jnp.max(sg, axis=0, keepdims=True) + bg)
                tiles.append(sg)
                shifts.append(bg)
            m_ref[i] = m_new
            m_new = m_new + tok
            for g in range(n_live):
                shift = m_new if shifts[g] is None else m_new - shifts[g]
                p_ref[g * BLK:(g + 1) * BLK, :] = jnp.exp2(tiles[g] - shift).astype(BF16)
            return m_new
        return probs

    def run_items(count, item, probs, tok0, n_live):
        def clamped(t):
            return item(jnp.clip(t, 0, count - 1))

        scores(*clamped(0), s_bufs[0], n_live)
        scores(*clamped(1), s_bufs[1], n_live)

        def slot(t, k, carry):
            m_prev, toks = carry
            scores(*clamped(t + 2), s_bufs[(k + 2) % 4], n_live)
            i_prev, st_prev = clamped(t - 1)
            pv = values(st_prev, p_bufs[(k - 1) % 4], n_live)
            i, st = clamped(t)
            m_t = probs(i, st, s_bufs[k % 4], p_bufs[k % 4], toks[0])
            fold(pv, m_prev, i_prev)
            return m_t, (toks[1], toks[2], pv[2] * 0.0)

        unroll = 4

        def body(it, carry):
            for k in range(unroll):
                carry = slot(unroll * it + k, k, carry)
            return carry

        m_last, _ = lax.fori_loop(0, count // unroll, body, (r_ref[clamped(0)[0]], (tok0, tok0, tok0)))
        i_last, st_last = clamped(count - 1)
        pv_last = values(st_last, p_bufs[(count - 1) % 4], n_live)
        fold(pv_last, m_last, i_last)
        tok_last = pv_last[2] * 0.0
        p_bufs[3][...] = jnp.broadcast_to(tok_last.astype(BF16), p_bufs[3].shape)
        return tok_last

    p_bufs[3][...] = jnp.zeros_like(p_bufs[3])
    tok = jnp.zeros((1, 2 * BLK), F32)
    for own in range(G):
        tok = run_items(nb // G, lambda u, own=own: (G * u + own, u), make_probs(own), tok, own + 1)
    run_items(n_items, lambda t: (tile_tab_ref[t], stage_tab_ref[t]), make_probs(None), tok, G)

    def finish(i, c):
        l = l_ref[i]
        acc = acc_ref[i]
        out_t = jnp.concatenate([acc[:DH] / l[:, :BLK], acc[DH:] / l[:, BLK:]], axis=0)
        o_ref[pl.ds(pl.multiple_of(i * BLK, BLK), BLK), :] = out_t.T.astype(BF16)
        return c

    lax.fori_loop(0, nb, finish, 0)


def _moba(mq, mk, mvt, *, batch, seq):
    BLK = MOBA_BLOCK
    nb = seq // BLK
    G = MOBA_STAGE
    assert nb % (4 * G) == 0
    npair = MOBA_W // LANES
    items = [(i, st) for st in range(nb // G) for i in range(G * st + G, nb)]
    assert len(items) % 4 == 0
    tile_tab = jnp.asarray([i for i, _ in items], jnp.int32)
    stage_tab = jnp.asarray([st for _, st in items], jnp.int32)
    qkspec = pl.BlockSpec((seq, LANES), lambda b, hp, *_: (b, hp))
    vtspec = pl.BlockSpec((1, 1, nb, LANES, BLK), lambda b, hp, *_: (b, hp, 0, 0, 0))
    stat = pltpu.VMEM((nb, 1, 2 * BLK), F32)
    sbuf = pltpu.VMEM((G * BLK, 2 * BLK), F32)
    pbuf = pltpu.VMEM((G * BLK, 2 * BLK), BF16)
    return pl.pallas_call(
        functools.partial(_moba_kernel, n_items=len(items)),
        grid_spec=pltpu.PrefetchScalarGridSpec(
            num_scalar_prefetch=2,
            grid=(batch, npair),
            in_specs=[qkspec, qkspec, vtspec],
            out_specs=qkspec,
            scratch_shapes=[pltpu.VMEM((nb, LANES), F32), pltpu.VMEM((nb, nb, 2 * BLK), F32),
                            pltpu.VMEM((nb, 2 * BLK, LANES), BF16),
                            stat, stat, stat, pltpu.VMEM((nb, LANES, BLK), F32),
                            sbuf, sbuf, sbuf, sbuf, pbuf, pbuf, pbuf, pbuf]),
        out_shape=jax.ShapeDtypeStruct((batch * seq, MOBA_W), BF16),
        compiler_params=pltpu.CompilerParams(dimension_semantics=("arbitrary",) * 2,
                                             vmem_limit_bytes=VMEM_LIMIT),
        name="moba",
    )(tile_tab, stage_tab, mq, mk, mvt)


def _merge_kernel(x_ref, ret_ref, moba_ref, ga_ref, gb_ref, wr_ref, wm_ref, wo_ref, o_ref):
    a = _dot(ret_ref[...], wr_ref[...])
    b = _dot(moba_ref[...], wm_ref[...])
    mix = ga_ref[...].astype(F32) * a + gb_ref[...].astype(F32) * b
    o_ref[...] = x_ref[...] + _dot(mix.astype(BF16), wo_ref[...])


def _merge(x2, ret, moba, sga, sgb, w_ret_out, w_moba_out, w_o, *, tm):
    T = x2.shape[0]
    row = lambda w: pl.BlockSpec((tm, w), lambda i: (i, 0))
    return pl.pallas_call(
        _merge_kernel,
        grid=(T // tm,),
        in_specs=[row(D_MODEL), row(RET_V), row(MOBA_W), row(D_MODEL), row(D_MODEL),
                  _const_spec((RET_V, D_MODEL)), _const_spec((MOBA_W, D_MODEL)), _const_spec((D_MODEL, D_MODEL))],
        out_specs=row(D_MODEL),
        out_shape=jax.ShapeDtypeStruct((T, D_MODEL), F32),
        compiler_params=pltpu.CompilerParams(dimension_semantics=("arbitrary",), vmem_limit_bytes=VMEM_LIMIT),
        name="merge",
    )(x2, ret, moba, sga, sgb, w_ret_out.astype(BF16), w_moba_out.astype(BF16), w_o.astype(BF16))


def _ffn_kernel(x_ref, n2_ref, wg_ref, wu_ref, wd_ref, o_ref):
    x = x_ref[...]
    ms = jnp.mean(x * x, axis=-1, keepdims=True)
    h = (x * lax.rsqrt(ms + RMS_EPS) * n2_ref[...]).astype(BF16)
    out = x
    for c0, c1 in FFN_COL_CHUNKS:
        g = _dot(h, wg_ref[:, c0:c1])
        u = _dot(h, wu_ref[:, c0:c1])
        act = (g * jax.nn.sigmoid(g) * u).astype(BF16)
        out = out + _dot(act, wd_ref[c0:c1, :])
    o_ref[...] = out


def _ffn(x2, norm2_w, w_gate, w_up, w_down, *, tm):
    T = x2.shape[0]
    row = pl.BlockSpec((tm, D_MODEL), lambda i: (i, 0))
    return pl.pallas_call(
        _ffn_kernel,
        grid=(T // tm,),
        in_specs=[row, _const_spec((1, D_MODEL)), _const_spec((D_MODEL, FFN_HIDDEN)),
                  _const_spec((D_MODEL, FFN_HIDDEN)), _const_spec((FFN_HIDDEN, D_MODEL))],
        out_specs=row,
        out_shape=jax.ShapeDtypeStruct((T, D_MODEL), F32),
        compiler_params=pltpu.CompilerParams(dimension_semantics=("arbitrary",), vmem_limit_bytes=VMEM_LIMIT),
        name="ffn",
    )(x2, norm2_w.astype(F32)[None, :], w_gate.astype(BF16), w_up.astype(BF16), w_down.astype(BF16))


def kernel(x, norm1_w, w_in, q_norm_w, k_norm_w, w_ret_out, w_moba_out, w_o, norm2_w, w_ffn_gate, w_ffn_up, w_ffn_down):
    B, S, D = x.shape
    assert D == D_MODEL and S % MOBA_BLOCK == 0 and S % MERGE_ROW_TILE == 0 and S % ROW_TILE == 0
    depth = norm1_w.shape[0]
    x2 = x.reshape(B * S, D)
    for l in range(depth):
        rq, rk, rv, rgs, mq, mk, mvt, sga, sgb = _inproj(
            x2, norm1_w[l], w_in[l], q_norm_w[l], k_norm_w[l], seq=S, tm=ROW_TILE)
        ret = _retention(rq, rk, rv, rgs, batch=B, seq=S, rt=ROW_TILE)
        moba = _moba(mq, mk, mvt, batch=B, seq=S)
        x2 = _merge(x2, ret, moba, sga, sgb, w_ret_out[l], w_moba_out[l], w_o[l], tm=MERGE_ROW_TILE)
        x2 = _ffn(x2, norm2_w[l], w_ffn_gate[l], w_ffn_up[l], w_ffn_down[l], tm=FFN_ROW_TILE)
    return x2.reshape(B, S, D)
```

```python
import functools
import math

import numpy as np
import jax
import jax.numpy as jnp
from jax import lax
from jax.experimental import pallas as pl
from jax.experimental.pallas import tpu as pltpu

D_MODEL = 1024
RET_HEADS = 4
RET_DK = 128
RET_DV = 256
ROPE_BASE = 10000.0
MOBA_HEADS = 8
MOBA_DH = 64
MOBA_BLOCK = 256
MOBA_TOPK = 3
FFN_HIDDEN = 2816
RET_QK = RET_HEADS * RET_DK
RET_V = RET_HEADS * RET_DV
MOBA_W = MOBA_HEADS * MOBA_DH
IN_SPLITS = (RET_QK, RET_QK, RET_V, RET_V, MOBA_W, MOBA_W, MOBA_W, D_MODEL, D_MODEL)
IN_COLS = sum(IN_SPLITS)
RMS_EPS = 1e-6
GN_EPS = 1e-5
NEG = -1e30

LANES = 128
BF16_ROWS = 16
VMEM_LIMIT = 52 * 1024 * 1024
RET_CHUNK = 256
MOBA_STAGE = 4
ROW_TILE = 512
MERGE_ROW_TILE = 1024
FFN_ROW_TILE = 1024
FFN_COL_CHUNKS = ((0, 1536), (1536, FFN_HIDDEN))

F32 = jnp.float32
BF16 = jnp.bfloat16


def _dot(a, b):
    return jnp.dot(a, b, preferred_element_type=F32)


def _dot_nt(a, b):
    return lax.dot_general(a, b, (((1,), (1,)), ((), ())), preferred_element_type=F32)


def _dot_tn(a, b):
    return lax.dot_general(a, b, (((0,), (0,)), ((), ())), preferred_element_type=F32)


def _const_spec(shape):
    nd = len(shape)
    return pl.BlockSpec(shape, lambda *_: (0,) * nd, pipeline_mode=pl.Buffered(1))


def _inproj_kernel(x_ref, xn_ref, n1_ref, w_ref, qn_ref, kn_ref, cos_ref, sin_ref,
                   rq_ref, rk_ref, rv_ref, rg_ref, mq_ref, mk_ref, mvt_ref, ga_ref, gb_ref, h_ref,
                   *, tm):
    i = pl.program_id(0)

    def normed(x):
        ms = jnp.mean(x * x, axis=-1, keepdims=True)
        return (x * lax.rsqrt(ms + RMS_EPS) * n1_ref[...]).astype(BF16)

    @pl.when(i == 0)
    def _():
        h_ref[0] = normed(x_ref[...])

    h = h_ref[i % 2]

    offs = np.cumsum((0,) + IN_SPLITS)

    def proj(n):
        return _dot(h, w_ref[:, int(offs[n]):int(offs[n + 1])])

    cos = cos_ref[...]
    sin = sin_ref[...]
    lane = lax.broadcasted_iota(jnp.int32, (tm, LANES), 1)

    def rotary_store(p, out_ref, scale):
        for hd in range(RET_HEADS):
            xh = p[:, hd * RET_DK:(hd + 1) * RET_DK]
            y = xh * cos + pltpu.roll(xh, RET_DK // 2, 1) * sin
            if scale is not None:
                y = y * scale
            out_ref[:, hd * RET_DK:(hd + 1) * RET_DK] = y.astype(BF16)

    rotary_store(proj(0), rq_ref, None)
    rotary_store(proj(1), rk_ref, RET_DK ** -0.5)
    rv_ref[...] = proj(2).astype(BF16)
    rg = proj(3)
    rg_ref[...] = (rg * jax.nn.sigmoid(rg)).astype(BF16)

    low_half = lane < MOBA_DH

    def head_rms_store(p, w_row_ref, out_ref, scale):
        for g in range(MOBA_W // LANES):
            cols = slice(g * LANES, (g + 1) * LANES)
            ph = p[:, cols]
            sq = ph * ph
            lo = jnp.sum(jnp.where(low_half, sq, 0.0), axis=-1, keepdims=True)
            hi = jnp.sum(jnp.where(low_half, 0.0, sq), axis=-1, keepdims=True)
            msq = jnp.where(low_half, lo, hi) * (1.0 / MOBA_DH)
            y = ph * lax.rsqrt(msq + RMS_EPS) * w_row_ref[:, cols]
            out_ref[:, cols] = (y if scale is None else y * scale).astype(BF16)

    head_rms_store(proj(4), qn_ref, mq_ref, MOBA_DH ** -0.5 * math.log2(math.e))
    head_rms_store(proj(5), kn_ref, mk_ref, None)
    mvt = proj(6).T.astype(BF16)
    for pair in range(MOBA_W // LANES):
        for c in range(tm // MOBA_BLOCK):
            mvt_ref[0, pair, c] = mvt[pair * LANES:(pair + 1) * LANES, c * MOBA_BLOCK:(c + 1) * MOBA_BLOCK]
    ga_ref[...] = jax.nn.sigmoid(proj(7)).astype(BF16)
    gb_ref[...] = jax.nn.sigmoid(proj(8)).astype(BF16)
    h_ref[(i + 1) % 2] = normed(xn_ref[...])


def _inproj(x2, norm1_w, w_in, q_norm_w, k_norm_w, *, seq, tm):
    T = x2.shape[0]
    half = RET_DK // 2
    inv = ROPE_BASE ** (-np.arange(half, dtype=np.float64) / half)
    ang = np.arange(seq, dtype=np.float64)[:, None] * inv[None, :]
    cos2 = jnp.asarray(np.concatenate([np.cos(ang), np.cos(ang)], axis=1), F32)
    sin2 = jnp.asarray(np.concatenate([-np.sin(ang), np.sin(ang)], axis=1), F32)
    qn = jnp.tile(q_norm_w.astype(F32), MOBA_HEADS)[None, :]
    kn = jnp.tile(k_norm_w.astype(F32), MOBA_HEADS)[None, :]
    offs = np.cumsum((0,) + IN_SPLITS)
    n_tiles = T // tm
    nt = seq // tm
    npair = MOBA_W // LANES
    nb = seq // MOBA_BLOCK
    row = lambda w: pl.BlockSpec((tm, w), lambda i: (i, 0))
    next_row = pl.BlockSpec((tm, D_MODEL), lambda i: (jnp.minimum(i + 1, n_tiles - 1), 0))
    pos_row = pl.BlockSpec((tm, LANES), lambda i: (i % nt, 0))
    row_out = lambda w: (row(w), jax.ShapeDtypeStruct((T, w), BF16))
    mvt_out = (pl.BlockSpec((1, npair, tm // MOBA_BLOCK, LANES, MOBA_BLOCK), lambda i: (i // nt, 0, i % nt, 0, 0)),
               jax.ShapeDtypeStruct((T // seq, npair, nb, LANES, MOBA_BLOCK), BF16))
    outs = [row_out(RET_QK), row_out(RET_QK), row_out(RET_V), row_out(RET_V), row_out(MOBA_W), row_out(MOBA_W),
            mvt_out, row_out(D_MODEL), row_out(D_MODEL)]
    return pl.pallas_call(
        functools.partial(_inproj_kernel, tm=tm),
        grid=(n_tiles,),
        in_specs=[row(D_MODEL), next_row, _const_spec((1, D_MODEL)), _const_spec((D_MODEL, IN_COLS)),
                  _const_spec((1, MOBA_W)), _const_spec((1, MOBA_W)), pos_row, pos_row],
        out_specs=[o[0] for o in outs],
        out_shape=[o[1] for o in outs],
        scratch_shapes=[pltpu.VMEM((2, tm, D_MODEL), BF16)],
        compiler_params=pltpu.CompilerParams(dimension_semantics=("arbitrary",), vmem_limit_bytes=VMEM_LIMIT),
        name="inproj",
    )(x2, x2, norm1_w.astype(F32)[None, :], w_in.astype(BF16), qn, kn, cos2, sin2)


def _retention_kernel(q_ref, k_ref, v_ref, g_ref, dmat_ref, xi_ref, zeta_ref, o_ref, state_ref,
                      *, rt, chunk, g_chunk):
    @pl.when(pl.program_id(1) == 0)
    def _():
        state_ref[...] = jnp.zeros_like(state_ref)

    for c in range(rt // chunk):
        rows = slice(c * chunk, (c + 1) * chunk)
        for hd in range(RET_HEADS):
            kcols = slice(hd * RET_DK, (hd + 1) * RET_DK)
            vcols = slice(hd * RET_DV, (hd + 1) * RET_DV)
            q = q_ref[rows, kcols]
            k = k_ref[rows, kcols]
            v = v_ref[rows, vcols]
            state = state_ref[hd]
            scores = _dot_nt(q, k) * dmat_ref[hd]
            q_dec = (q.astype(F32) * xi_ref[hd]).astype(BF16)
            o = _dot(scores.astype(BF16), v) + _dot(q_dec, state.astype(BF16))
            k_dec = (k.astype(F32) * zeta_ref[hd]).astype(BF16)
            state_ref[hd] = g_chunk[hd] * state + _dot_tn(k_dec, v)
            mu = jnp.mean(o, axis=-1, keepdims=True)
            d = o - mu
            var = jnp.mean(d * d, axis=-1, keepdims=True)
            y = d * lax.rsqrt(var + GN_EPS)
            o_ref[rows, vcols] = (g_ref[rows, vcols].astype(F32) * y).astype(BF16)


def _retention(rq, rk, rv, rgs, *, batch, seq, rt):
    C = RET_CHUNK
    lg = np.log1p(-np.exp2(-5.0 - np.arange(RET_HEADS, dtype=np.float64)))
    idx = np.arange(C, dtype=np.float64)
    diff = idx[:, None] - idx[None, :]
    dmat = np.where(diff >= 0, np.exp(np.maximum(diff, 0.0)[None] * lg[:, None, None]), 0.0)
    xi = np.broadcast_to(np.exp((idx + 1.0)[None, :] * lg[:, None])[:, :, None], (RET_HEADS, C, LANES))
    zeta = np.broadcast_to(np.exp((C - 1.0 - idx)[None, :] * lg[:, None])[:, :, None], (RET_HEADS, C, LANES))
    g_chunk = tuple(float(v) for v in np.exp(C * lg))
    nt = seq // rt
    row = lambda w: pl.BlockSpec((rt, w), lambda b, j: (b * nt + j, 0))
    return pl.pallas_call(
        functools.partial(_retention_kernel, rt=rt, chunk=C, g_chunk=g_chunk),
        grid=(batch, nt),
        in_specs=[row(RET_QK), row(RET_QK), row(RET_V), row(RET_V),
                  _const_spec((RET_HEADS, C, C)), _const_spec((RET_HEADS, C, LANES)),
                  _const_spec((RET_HEADS, C, LANES))],
        out_specs=row(RET_V),
        out_shape=jax.ShapeDtypeStruct((batch * seq, RET_V), BF16),
        scratch_shapes=[pltpu.VMEM((RET_HEADS, RET_DK, RET_DV), F32)],
        compiler_params=pltpu.CompilerParams(dimension_semantics=("arbitrary", "arbitrary"),
                                             vmem_limit_bytes=VMEM_LIMIT),
        name="retention",
    )(rq, rk, rv, rgs, jnp.asarray(dmat, F32), jnp.asarray(xi, F32), jnp.asarray(zeta, F32))


def _moba_kernel(tile_tab_ref, stage_tab_ref, q_ref, k_ref, vt_ref, o_ref,
                 kbar_ref, bias_ref, qcat_ref, m_ref, r_ref, l_ref, acc_ref,
                 s0_ref, s1_ref, s2_ref, s3_ref, p0_ref, p1_ref, p2_ref, p3_ref, *, n_items):
    BLK = MOBA_BLOCK
    DH = MOBA_DH
    G = MOBA_STAGE
    STG = G * BLK
    nb = vt_ref.shape[2]
    nbp = kbar_ref.shape[0]
    s_bufs = (s0_ref, s1_ref, s2_ref, s3_ref)
    p_bufs = (p0_ref, p1_ref, p2_ref, p3_ref)
    lane = lax.broadcasted_iota(jnp.int32, (1, LANES), 1)

    def q_cat_body(i, c):
        q_pair = q_ref[pl.ds(pl.multiple_of(i * BLK, BLK), BLK), :]
        zero = jnp.zeros_like(q_pair)
        qcat_ref[i, :BLK, :] = jnp.where(lane < DH, q_pair, zero)
        qcat_ref[i, BLK:, :] = jnp.where(lane >= DH, q_pair, zero)
        return c

    lax.fori_loop(0, nb, q_cat_body, 0, unroll=4)

    def q_cat_of(i):
        return qcat_ref[i]

    kbar_ref[...] = jnp.zeros_like(kbar_ref)

    def kbar_body(n, c):
        kb = k_ref[pl.ds(pl.multiple_of(n * BLK, BLK), BLK), :]
        kbar_ref[pl.ds(n, 1), :] = jnp.mean(kb.astype(F32), axis=0, keepdims=True)
        return c

    lax.fori_loop(0, nb, kbar_body, 0, unroll=4)

    kbar = kbar_ref[...]
    kbar_hi = kbar.astype(BF16)
    kbar_lo = (kbar - kbar_hi.astype(F32)).astype(BF16)
    blk_idx = lax.broadcasted_iota(jnp.int32, (nbp, 2 * BLK), 0)

    def select_blocks(i):
        q_cat = q_cat_of(i)
        gate = _dot_nt(kbar_hi, q_cat) + _dot_nt(kbar_lo, q_cat)
        past = blk_idx < i
        gate = jnp.where(past, gate, NEG)
        sel = jnp.zeros((nbp, 2 * BLK), jnp.bool_)
        for _ in range(MOBA_TOPK):
            best = jnp.max(gate, axis=0, keepdims=True)
            first = jnp.min(jnp.where(gate == best, blk_idx, nbp), axis=0, keepdims=True)
            pick = blk_idx == first
            sel = sel | (pick & past)
            gate = jnp.where(pick, -3e38, gate)
        bias_ref[i] = jnp.where(sel, 0.0, NEG)

    def select_body(u, c):
        select_blocks(2 * u)
        select_blocks(2 * u + 1)
        return c

    lax.fori_loop(0, nb // 2, select_body, 0, unroll=2)

    m_ref[...] = jnp.full(m_ref.shape, NEG, F32)
    r_ref[...] = jnp.full(r_ref.shape, NEG, F32)
    l_ref[...] = jnp.zeros_like(l_ref)
    acc_ref[...] = jnp.zeros_like(acc_ref)

    def scores(i, st, dst_ref, n_live):
        rows = n_live * BLK
        ks = k_ref[pl.ds(pl.multiple_of(st * STG, STG), rows), :]
        dst_ref[:rows, :] = _dot_nt(ks, q_cat_of(i))

    def values(st, p_ref, n_live):
        rows = n_live * BLK
        vt = jnp.concatenate([vt_ref[0, 0, G * st + g] for g in range(n_live)], axis=1)
        pv = _dot(jnp.concatenate([vt, jnp.ones((BF16_ROWS, rows), BF16)], axis=0), p_ref[:rows, :])
        return pv[:DH, :BLK], pv[DH:2 * DH, BLK:], pv[2 * DH:2 * DH + 1, :]

    def fold(pv, m_pv, i):
        num0, num1, sums = pv
        scale = jnp.exp2(r_ref[i] - m_pv)
        r_ref[i] = m_pv
        l_ref[i] = scale * l_ref[i] + sums
        acc_ref[i, :DH, :] = scale[:, :BLK] * acc_ref[i, :DH, :] + num0
        acc_ref[i, DH:, :] = scale[:, BLK:] * acc_ref[i, DH:, :] + num1

    key_idx = lax.broadcasted_iota(jnp.int32, (BLK, 2 * BLK), 0)
    qry_idx = lax.broadcasted_iota(jnp.int32, (BLK, 2 * BLK), 1) & (BLK - 1)
    causal = key_idx <= qry_idx

    def make_probs(own):
        def probs(i, st, s_ref, p_ref, tok):
            n_live = G if own is None else own + 1
            tiles, shifts = [], []
            m_new = m_ref[i]
            for g in range(n_live):
                sg = s_ref[g * BLK:(g + 1) * BLK, :]
                if g == own:
                    sg = jnp.where(causal, sg, NEG)
                    bg = None
                    m_new = jnp.maximum(m_new, jnp.max(sg, axis=0, keepdims=True))
                else:
                    bg = bias_ref[i, pl.ds(G * st + g, 1), :]
                    m_new = jnp.maximum(m_new, jnp.max(sg, axis=0, keepdims=True) + bg)
                tiles.append(sg)
                shifts.append(bg)
            m_ref[i] = m_new
            m_new = m_new + tok
            for g in range(n_live):
                shift = m_new if shifts[g] is None else m_new - shifts[g]
                p_ref[g * BLK:(g + 1) * BLK, :] = jnp.exp2(tiles[g] - shift).astype(BF16)
            return m_new
        return probs

    def run_items(count, item, probs, tok0, n_live):
        def clamped(t):
            return item(jnp.clip(t, 0, count - 1))

        scores(*clamped(0), s_bufs[0], n_live)
        scores(*clamped(1), s_bufs[1], n_live)

        def slot(t, k, carry):
            m_prev, toks = carry
            scores(*clamped(t + 2), s_bufs[(k + 2) % 4], n_live)
            i_prev, st_prev = clamped(t - 1)
            pv = values(st_prev, p_bufs[(k - 1) % 4], n_live)
            i, st = clamped(t)
            m_t = probs(i, st, s_bufs[k % 4], p_bufs[k % 4], toks[0])
            fold(pv, m_prev, i_prev)
            return m_t, (toks[1], toks[2], pv[2] * 0.0)

        unroll = 4

        def body(it, carry):
            for k in range(unroll):
                carry = slot(unroll * it + k, k, carry)
            return carry

        m_last, _ = lax.fori_loop(0, count // unroll, body, (r_ref[clamped(0)[0]], (tok0, tok0, tok0)))
        i_last, st_last = clamped(count - 1)
        pv_last = values(st_last, p_bufs[(count - 1) % 4], n_live)
        fold(pv_last, m_last, i_last)
        tok_last = pv_last[2] * 0.0
        p_bufs[3][...] = jnp.broadcast_to(tok_last.astype(BF16), p_bufs[3].shape)
        return tok_last

    p_bufs[3][...] = jnp.zeros_like(p_bufs[3])
    tok = jnp.zeros((1, 2 * BLK), F32)
    for own in range(G):
        tok = run_items(nb // G, lambda u, own=own: (G * u + own, u), make_probs(own), tok, own + 1)
    run_items(n_items, lambda t: (tile_tab_ref[t], stage_tab_ref[t]), make_probs(None), tok, G)

    def finish(i, c):
        l = l_ref[i]
        acc = acc_ref[i]
        out_t = jnp.concatenate([acc[:DH] / l[:, :BLK], acc[DH:] / l[:, BLK:]], axis=0)
        o_ref[pl.ds(pl.multiple_of(i * BLK, BLK), BLK), :] = out_t.T.astype(BF16)
        return c

    lax.fori_loop(0, nb, finish, 0, unroll=4)


def _moba(mq, mk, mvt, *, batch, seq):
    BLK = MOBA_BLOCK
    nb = seq // BLK
    G = MOBA_STAGE
    assert nb % (4 * G) == 0
    npair = MOBA_W // LANES
    items = [(i, st) for st in range(nb // G) for i in range(G * st + G, nb)]
    assert len(items) % 4 == 0
    tile_tab = jnp.asarray([i for i, _ in items], jnp.int32)
    stage_tab = jnp.asarray([st for _, st in items], jnp.int32)
    qkspec = pl.BlockSpec((seq, LANES), lambda b, hp, *_: (b, hp))
    vtspec = pl.BlockSpec((1, 1, nb, LANES, BLK), lambda b, hp, *_: (b, hp, 0, 0, 0))
    stat = pltpu.VMEM((nb, 1, 2 * BLK), F32)
    sbuf = pltpu.VMEM((G * BLK, 2 * BLK), F32)
    pbuf = pltpu.VMEM((G * BLK, 2 * BLK), BF16)
    return pl.pallas_call(
        functools.partial(_moba_kernel, n_items=len(items)),
        grid_spec=pltpu.PrefetchScalarGridSpec(
            num_scalar_prefetch=2,
            grid=(batch, npair),
            in_specs=[qkspec, qkspec, vtspec],
            out_specs=qkspec,
            scratch_shapes=[pltpu.VMEM((nb, LANES), F32), pltpu.VMEM((nb, nb, 2 * BLK), F32),
                            pltpu.VMEM((nb, 2 * BLK, LANES), BF16),
                            stat, stat, stat, pltpu.VMEM((nb, LANES, BLK), F32),
                            sbuf, sbuf, sbuf, sbuf, pbuf, pbuf, pbuf, pbuf]),
        out_shape=jax.ShapeDtypeStruct((batch * seq, MOBA_W), BF16),
        compiler_params=pltpu.CompilerParams(dimension_semantics=("arbitrary",) * 2,
                                             vmem_limit_bytes=VMEM_LIMIT),
        name="moba",
    )(tile_tab, stage_tab, mq, mk, mvt)


def _merge_kernel(x_ref, ret_ref, moba_ref, ga_ref, gb_ref, wr_ref, wm_ref, wo_ref, o_ref):
    a = _dot(ret_ref[...], wr_ref[...])
    b = _dot(moba_ref[...], wm_ref[...])
    mix = ga_ref[...].astype(F32) * a + gb_ref[...].astype(F32) * b
    o_ref[...] = x_ref[...] + _dot(mix.astype(BF16), wo_ref[...])


def _merge(x2, ret, moba, sga, sgb, w_ret_out, w_moba_out, w_o, *, tm):
    T = x2.shape[0]
    row = lambda w: pl.BlockSpec((tm, w), lambda i: (i, 0))
    return pl.pallas_call(
        _merge_kernel,
        grid=(T // tm,),
        in_specs=[row(D_MODEL), row(RET_V), row(MOBA_W), row(D_MODEL), row(D_MODEL),
                  _const_spec((RET_V, D_MODEL)), _const_spec((MOBA_W, D_MODEL)), _const_spec((D_MODEL, D_MODEL))],
        out_specs=row(D_MODEL),
        out_shape=jax.ShapeDtypeStruct((T, D_MODEL), F32),
        compiler_params=pltpu.CompilerParams(dimension_semantics=("arbitrary",), vmem_limit_bytes=VMEM_LIMIT),
        name="merge",
    )(x2, ret, moba, sga, sgb, w_ret_out.astype(BF16), w_moba_out.astype(BF16), w_o.astype(BF16))


def _ffn_kernel(x_ref, n2_ref, wg_ref, wu_ref, wd_ref, o_ref):
    x = x_ref[...]
    ms = jnp.mean(x * x, axis=-1, keepdims=True)
    h = (x * lax.rsqrt(ms + RMS_EPS) * n2_ref[...]).astype(BF16)
    out = x
    for c0, c1 in FFN_COL_CHUNKS:
        g = _dot(h, wg_ref[:, c0:c1])
        u = _dot(h, wu_ref[:, c0:c1])
        act = (g * jax.nn.sigmoid(g) * u).astype(BF16)
        out = out + _dot(act, wd_ref[c0:c1, :])
    o_ref[...] = out


def _ffn(x2, norm2_w, w_gate, w_up, w_down, *, tm):
    T = x2.shape[0]
    row = pl.BlockSpec((tm, D_MODEL), lambda i: (i, 0))
    return pl.pallas_call(
        _ffn_kernel,
        grid=(T // tm,),
        in_specs=[row, _const_spec((1, D_MODEL)), _const_spec((D_MODEL, FFN_HIDDEN)),
                  _const_spec((D_MODEL, FFN_HIDDEN)), _const_spec((FFN_HIDDEN, D_MODEL))],
        out_specs=row,
        out_shape=jax.ShapeDtypeStruct((T, D_MODEL), F32),
        compiler_params=pltpu.CompilerParams(dimension_semantics=("arbitrary",), vmem_limit_bytes=VMEM_LIMIT),
        name="ffn",
    )(x2, norm2_w.astype(F32)[None, :], w_gate.astype(BF16), w_up.astype(BF16), w_down.astype(BF16))


def kernel(x, norm1_w, w_in, q_norm_w, k_norm_w, w_ret_out, w_moba_out, w_o, norm2_w, w_ffn_gate, w_ffn_up, w_ffn_down):
    B, S, D = x.shape
    assert D == D_MODEL and S % MOBA_BLOCK == 0 and S % MERGE_ROW_TILE == 0 and S % ROW_TILE == 0
    depth = norm1_w.shape[0]
    x2 = x.reshape(B * S, D)
    for l in range(depth):
        rq, rk, rv, rgs, mq, mk, mvt, sga, sgb = _inproj(
            x2, norm1_w[l], w_in[l], q_norm_w[l], k_norm_w[l], seq=S, tm=ROW_TILE)
        ret = _retention(rq, rk, rv, rgs, batch=B, seq=S, rt=ROW_TILE)
        moba = _moba(mq, mk, mvt, batch=B, seq=S)
        x2 = _merge(x2, ret, moba, sga, sgb, w_ret_out[l], w_moba_out[l], w_o[l], tm=MERGE_ROW_TILE)
        x2 = _ffn(x2, norm2_w[l], w_ffn_gate[l], w_ffn_up[l], w_ffn_down[l], tm=FFN_ROW_TILE)
    return x2.reshape(B, S, D)
```

```python
import functools
import math

import numpy as np
import jax
import jax.numpy as jnp
from jax import lax
from jax.experimental import pallas as pl
from jax.experimental.pallas import tpu as pltpu

D_MODEL = 1024
RET_HEADS = 4
RET_DK = 128
RET_DV = 256
ROPE_BASE = 10000.0
MOBA_HEADS = 8
MOBA_DH = 64
MOBA_BLOCK = 256
MOBA_TOPK = 3
FFN_HIDDEN = 2816
RET_QK = RET_HEADS * RET_DK
RET_V = RET_HEADS * RET_DV
MOBA_W = MOBA_HEADS * MOBA_DH
IN_SPLITS = (RET_QK, RET_QK, RET_V, RET_V, MOBA_W, MOBA_W, MOBA_W, D_MODEL, D_MODEL)
IN_COLS = sum(IN_SPLITS)
RMS_EPS = 1e-6
GN_EPS = 1e-5
NEG = -1e30

LANES = 128
BF16_ROWS = 16
VMEM_LIMIT = 52 * 1024 * 1024
RET_CHUNK = 256
MOBA_STAGE = 4
ROW_TILE = 512
MERGE_ROW_TILE = 1024
FFN_ROW_TILE = 1024
FFN_COL_CHUNKS = ((0, 1536), (1536, FFN_HIDDEN))

F32 = jnp.float32
BF16 = jnp.bfloat16


def _dot(a, b):
    return jnp.dot(a, b, preferred_element_type=F32)


def _dot_nt(a, b):
    return lax.dot_general(a, b, (((1,), (1,)), ((), ())), preferred_element_type=F32)


def _dot_tn(a, b):
    return lax.dot_general(a, b, (((0,), (0,)), ((), ())), preferred_element_type=F32)


def _const_spec(shape):
    nd = len(shape)
    return pl.BlockSpec(shape, lambda *_: (0,) * nd, pipeline_mode=pl.Buffered(1))


def _inproj_kernel(x_ref, xn_ref, n1_ref, w_ref, qn_ref, kn_ref, cos_ref, sin_ref,
                   rq_ref, rk_ref, rv_ref, rg_ref, mq_ref, mk_ref, mvt_ref, ga_ref, gb_ref, h_ref,
                   *, tm):
    i = pl.program_id(0)

    def normed(x):
        ms = jnp.mean(x * x, axis=-1, keepdims=True)
        return (x * lax.rsqrt(ms + RMS_EPS) * n1_ref[...]).astype(BF16)

    @pl.when(i == 0)
    def _():
        h_ref[0] = normed(x_ref[...])

    h = h_ref[i % 2]

    offs = np.cumsum((0,) + IN_SPLITS)

    def proj(n):
        return _dot(h, w_ref[:, int(offs[n]):int(offs[n + 1])])

    cos = cos_ref[...]
    sin = sin_ref[...]
    lane = lax.broadcasted_iota(jnp.int32, (tm, LANES), 1)

    def rotary_store(p, out_ref, scale):
        for hd in range(RET_HEADS):
            xh = p[:, hd * RET_DK:(hd + 1) * RET_DK]
            y = xh * cos + pltpu.roll(xh, RET_DK // 2, 1) * sin
            if scale is not None:
                y = y * scale
            out_ref[:, hd * RET_DK:(hd + 1) * RET_DK] = y.astype(BF16)

    rotary_store(proj(0), rq_ref, None)
    rotary_store(proj(1), rk_ref, RET_DK ** -0.5)
    rv_ref[...] = proj(2).astype(BF16)
    rg = proj(3)
    rg_ref[...] = (rg * jax.nn.sigmoid(rg)).astype(BF16)

    low_half = lane < MOBA_DH

    def head_rms_store(p, w_row_ref, out_ref, scale):
        for g in range(MOBA_W // LANES):
            cols = slice(g * LANES, (g + 1) * LANES)
            ph = p[:, cols]
            sq = ph * ph
            lo = jnp.sum(jnp.where(low_half, sq, 0.0), axis=-1, keepdims=True)
            hi = jnp.sum(jnp.where(low_half, 0.0, sq), axis=-1, keepdims=True)
            msq = jnp.where(low_half, lo, hi) * (1.0 / MOBA_DH)
            y = ph * lax.rsqrt(msq + RMS_EPS) * w_row_ref[:, cols]
            out_ref[:, cols] = (y if scale is None else y * scale).astype(BF16)

    head_rms_store(proj(4), qn_ref, mq_ref, MOBA_DH ** -0.5 * math.log2(math.e))
    head_rms_store(proj(5), kn_ref, mk_ref, None)
    mvt = proj(6).T.astype(BF16)
    for pair in range(MOBA_W // LANES):
        for c in range(tm // MOBA_BLOCK):
            mvt_ref[0, pair, c] = mvt[pair * LANES:(pair + 1) * LANES, c * MOBA_BLOCK:(c + 1) * MOBA_BLOCK]
    ga_ref[...] = jax.nn.sigmoid(proj(7)).astype(BF16)
    gb_ref[...] = jax.nn.sigmoid(proj(8)).astype(BF16)
    h_ref[(i + 1) % 2] = normed(xn_ref[...])


def _inproj(x2, norm1_w, w_in, q_norm_w, k_norm_w, *, seq, tm):
    T = x2.shape[0]
    half = RET_DK // 2
    inv = ROPE_BASE ** (-np.arange(half, dtype=np.float64) / half)
    ang = np.arange(seq, dtype=np.float64)[:, None] * inv[None, :]
    cos2 = jnp.asarray(np.concatenate([np.cos(ang), np.cos(ang)], axis=1), F32)
    sin2 = jnp.asarray(np.concatenate([-np.sin(ang), np.sin(ang)], axis=1), F32)
    qn = jnp.tile(q_norm_w.astype(F32), MOBA_HEADS)[None, :]
    kn = jnp.tile(k_norm_w.astype(F32), MOBA_HEADS)[None, :]
    offs = np.cumsum((0,) + IN_SPLITS)
    n_tiles = T // tm
    nt = seq // tm
    npair = MOBA_W // LANES
    nb = seq // MOBA_BLOCK
    row = lambda w: pl.BlockSpec((tm, w), lambda i: (i, 0))
    next_row = pl.BlockSpec((tm, D_MODEL), lambda i: (jnp.minimum(i + 1, n_tiles - 1), 0))
    pos_row = pl.BlockSpec((tm, LANES), lambda i: (i % nt, 0))
    row_out = lambda w: (row(w), jax.ShapeDtypeStruct((T, w), BF16))
    mvt_out = (pl.BlockSpec((1, npair, tm // MOBA_BLOCK, LANES, MOBA_BLOCK), lambda i: (i // nt, 0, i % nt, 0, 0)),
               jax.ShapeDtypeStruct((T // seq, npair, nb, LANES, MOBA_BLOCK), BF16))
    outs = [row_out(RET_QK), row_out(RET_QK), row_out(RET_V), row_out(RET_V), row_out(MOBA_W), row_out(MOBA_W),
            mvt_out, row_out(D_MODEL), row_out(D_MODEL)]
    return pl.pallas_call(
        functools.partial(_inproj_kernel, tm=tm),
        grid=(n_tiles,),
        in_specs=[row(D_MODEL), next_row, _const_spec((1, D_MODEL)), _const_spec((D_MODEL, IN_COLS)),
                  _const_spec((1, MOBA_W)), _const_spec((1, MOBA_W)), pos_row, pos_row],
        out_specs=[o[0] for o in outs],
        out_shape=[o[1] for o in outs],
        scratch_shapes=[pltpu.VMEM((2, tm, D_MODEL), BF16)],
        compiler_params=pltpu.CompilerParams(dimension_semantics=("arbitrary",), vmem_limit_bytes=VMEM_LIMIT),
        name="inproj",
    )(x2, x2, norm1_w.astype(F32)[None, :], w_in.astype(BF16), qn, kn, cos2, sin2)


def _retention_kernel(q_ref, k_ref, v_ref, g_ref, dmat_ref, xi_ref, zeta_ref, o_ref, state_ref,
                      *, rt, chunk, g_chunk):
    @pl.when(pl.program_id(1) == 0)
    def _():
        state_ref[...] = jnp.zeros_like(state_ref)

    for c in range(rt // chunk):
        rows = slice(c * chunk, (c + 1) * chunk)
        for hd in range(RET_HEADS):
            kcols = slice(hd * RET_DK, (hd + 1) * RET_DK)
            vcols = slice(hd * RET_DV, (hd + 1) * RET_DV)
            q = q_ref[rows, kcols]
            k = k_ref[rows, kcols]
            v = v_ref[rows, vcols]
            state = state_ref[hd]
            scores = _dot_nt(q, k) * dmat_ref[hd]
            q_dec = (q.astype(F32) * xi_ref[hd]).astype(BF16)
            o = _dot(scores.astype(BF16), v) + _dot(q_dec, state.astype(BF16))
            k_dec = (k.astype(F32) * zeta_ref[hd]).astype(BF16)
            state_ref[hd] = g_chunk[hd] * state + _dot_tn(k_dec, v)
            mu = jnp.mean(o, axis=-1, keepdims=True)
            d = o - mu
            var = jnp.mean(d * d, axis=-1, keepdims=True)
            y = d * lax.rsqrt(var + GN_EPS)
            o_ref[rows, vcols] = (g_ref[rows, vcols].astype(F32) * y).astype(BF16)


def _retention(rq, rk, rv, rgs, *, batch, seq, rt):
    C = RET_CHUNK
    lg = np.log1p(-np.exp2(-5.0 - np.arange(RET_HEADS, dtype=np.float64)))
    idx = np.arange(C, dtype=np.float64)
    diff = idx[:, None] - idx[None, :]
    dmat = np.where(diff >= 0, np.exp(np.maximum(diff, 0.0)[None] * lg[:, None, None]), 0.0)
    xi = np.broadcast_to(np.exp((idx + 1.0)[None, :] * lg[:, None])[:, :, None], (RET_HEADS, C, LANES))
    zeta = np.broadcast_to(np.exp((C - 1.0 - idx)[None, :] * lg[:, None])[:, :, None], (RET_HEADS, C, LANES))
    g_chunk = tuple(float(v) for v in np.exp(C * lg))
    nt = seq // rt
    row = lambda w: pl.BlockSpec((rt, w), lambda b, j: (b * nt + j, 0))
    return pl.pallas_call(
        functools.partial(_retention_kernel, rt=rt, chunk=C, g_chunk=g_chunk),
        grid=(batch, nt),
        in_specs=[row(RET_QK), row(RET_QK), row(RET_V), row(RET_V),
                  _const_spec((RET_HEADS, C, C)), _const_spec((RET_HEADS, C, LANES)),
                  _const_spec((RET_HEADS, C, LANES))],
        out_specs=row(RET_V),
        out_shape=jax.ShapeDtypeStruct((batch * seq, RET_V), BF16),
        scratch_shapes=[pltpu.VMEM((RET_HEADS, RET_DK, RET_DV), F32)],
        compiler_params=pltpu.CompilerParams(dimension_semantics=("arbitrary", "arbitrary"),
                                             vmem_limit_bytes=VMEM_LIMIT),
        name="retention",
    )(rq, rk, rv, rgs, jnp.asarray(dmat, F32), jnp.asarray(xi, F32), jnp.asarray(zeta, F32))


def _moba_kernel(tile_tab_ref, stage_tab_ref, q_ref, k_ref, vt_ref, o_ref,
                 kbar_ref, bias_ref, qcat_ref, m_ref, r_ref, l_ref, acc_ref,
                 s0_ref, s1_ref, s2_ref, s3_ref, p0_ref, p1_ref, p2_ref, p3_ref, *, n_items):
    BLK = MOBA_BLOCK
    DH = MOBA_DH
    G = MOBA_STAGE
    STG = G * BLK
    nb = vt_ref.shape[2]
    nbp = kbar_ref.shape[0]
    s_bufs = (s0_ref, s1_ref, s2_ref, s3_ref)
    p_bufs = (p0_ref, p1_ref, p2_ref, p3_ref)
    lane = lax.broadcasted_iota(jnp.int32, (1, LANES), 1)

    def q_cat_body(i, c):
        q_pair = q_ref[pl.ds(pl.multiple_of(i * BLK, BLK), BLK), :]
        zero = jnp.zeros_like(q_pair)
        qcat_ref[i, :BLK, :] = jnp.where(lane < DH, q_pair, zero)
        qcat_ref[i, BLK:, :] = jnp.where(lane >= DH, q_pair, zero)
        return c

    lax.fori_loop(0, nb, q_cat_body, 0, unroll=4)

    def q_cat_of(i):
        return qcat_ref[i]

    kbar_ref[...] = jnp.zeros_like(kbar_ref)

    def kbar_body(n, c):
        kb = k_ref[pl.ds(pl.multiple_of(n * BLK, BLK), BLK), :]
        kbar_ref[pl.ds(n, 1), :] = jnp.mean(kb.astype(F32), axis=0, keepdims=True)
        return c

    lax.fori_loop(0, nb, kbar_body, 0, unroll=4)

    kbar = kbar_ref[...]
    kbar_hi = kbar.astype(BF16)
    kbar_lo = (kbar - kbar_hi.astype(F32)).astype(BF16)
    blk_idx = lax.broadcasted_iota(jnp.int32, (nbp, 2 * BLK), 0)

    def select_blocks(i):
        q_cat = q_cat_of(i)
        gate = _dot_nt(kbar_hi, q_cat) + _dot_nt(kbar_lo, q_cat)
        past = blk_idx < i
        gate = jnp.where(past, gate, NEG)
        sel = jnp.zeros((nbp, 2 * BLK), jnp.bool_)
        for _ in range(MOBA_TOPK):
            best = jnp.max(gate, axis=0, keepdims=True)
            first = jnp.min(jnp.where(gate == best, blk_idx, nbp), axis=0, keepdims=True)
            pick = blk_idx == first
            sel = sel | (pick & past)
            gate = jnp.where(pick, -3e38, gate)
        bias_ref[i] = jnp.where(sel, 0.0, NEG)

    def select_body(u, c):
        select_blocks(2 * u)
        select_blocks(2 * u + 1)
        return c

    lax.fori_loop(0, nb // 2, select_body, 0, unroll=2)

    m_ref[...] = jnp.full(m_ref.shape, NEG, F32)
    r_ref[...] = jnp.full(r_ref.shape, NEG, F32)
    l_ref[...] = jnp.zeros_like(l_ref)
    acc_ref[...] = jnp.zeros_like(acc_ref)

    def scores(i, st, dst_ref, n_live):
        rows = n_live * BLK
        ks = k_ref[pl.ds(pl.multiple_of(st * STG, STG), rows), :]
        dst_ref[:rows, :] = _dot_nt(ks, q_cat_of(i))

    def values(st, p_ref, n_live):
        rows = n_live * BLK
        vt = jnp.concatenate([vt_ref[0, 0, G * st + g] for g in range(n_live)], axis=1)
        pv = _dot(jnp.concatenate([vt, jnp.ones((BF16_ROWS, rows), BF16)], axis=0), p_ref[:rows, :])
        return pv[:DH, :BLK], pv[DH:2 * DH, BLK:], pv[2 * DH:2 * DH + 1, :]

    def fold(pv, m_pv, i):
        num0, num1, sums = pv
        scale = jnp.exp2(r_ref[i] - m_pv)
        r_ref[i] = m_pv
        l_ref[i] = scale * l_ref[i] + sums
        acc_ref[i, :DH, :] = scale[:, :BLK] * acc_ref[i, :DH, :] + num0
        acc_ref[i, DH:, :] = scale[:, BLK:] * acc_ref[i, DH:, :] + num1

    key_idx = lax.broadcasted_iota(jnp.int32, (BLK, 2 * BLK), 0)
    qry_idx = lax.broadcasted_iota(jnp.int32, (BLK, 2 * BLK), 1) & (BLK - 1)
    causal = key_idx <= qry_idx

    def make_probs(own):
        def probs(i, st, s_ref, p_ref, tok):
            n_live = G if own is None else own + 1
            tiles, shifts = [], []
            m_new = m_ref[i]
            for g in range(n_live):
                sg = s_ref[g * BLK:(g + 1) * BLK, :]
                if g == own:
                    sg = jnp.where(causal, sg, NEG)
                    bg = None
                    m_new = jnp.maximum(m_new, jnp.max(sg, axis=0, keepdims=True))
                else:
                    bg = bias_ref[i, pl.ds(G * st + g, 1), :]
                    m_new = jnp.maximum(m_new, jnp.max(sg, axis=0, keepdims=True) + bg)
                tiles.append(sg)
                shifts.append(bg)
            m_ref[i] = m_new
            m_new = m_new + tok
            for g in range(n_live):
                shift = m_new if shifts[g] is None else m_new - shifts[g]
                p_ref[g * BLK:(g + 1) * BLK, :] = jnp.exp2(tiles[g] - shift).astype(BF16)
            return m_new
        return probs

    def run_items(count, item, probs, tok0, n_live):
        def clamped(t):
            return item(jnp.clip(t, 0, count - 1))

        scores(*clamped(0), s_bufs[0], n_live)
        scores(*clamped(1), s_bufs[1], n_live)

        def slot(t, k, carry, look_ahead):
            m_prev, toks = carry
            if look_ahead:
                scores(*clamped(t + 2), s_bufs[(k + 2) % 4], n_live)
            i_prev, st_prev = clamped(t - 1)
            pv = values(st_prev, p_bufs[(k - 1) % 4], n_live)
            i, st = clamped(t)
            m_t = probs(i, st, s_bufs[k % 4], p_bufs[k % 4], toks[0])
            fold(pv, m_prev, i_prev)
            return m_t, (toks[1], toks[2], pv[2] * 0.0)

        unroll = 4
        trips = count // unroll

        def body(it, carry, last=False):
            for k in range(unroll):
                carry = slot(unroll * it + k, k, carry, look_ahead=not (last and k + 2 >= unroll))
            return carry

        carry = lax.fori_loop(0, trips - 1, body, (r_ref[clamped(0)[0]], (tok0, tok0, tok0)))
        m_last, _ = body(trips - 1, carry, last=True)
        i_last, st_last = clamped(count - 1)
        pv_last = values(st_last, p_bufs[(count - 1) % 4], n_live)
        fold(pv_last, m_last, i_last)
        tok_last = pv_last[2] * 0.0
        p_bufs[3][...] = jnp.broadcast_to(tok_last.astype(BF16), p_bufs[3].shape)
        return tok_last

    p_bufs[3][...] = jnp.zeros_like(p_bufs[3])
    tok = jnp.zeros((1, 2 * BLK), F32)
    for own in range(G):
        tok = run_items(nb // G, lambda u, own=own: (G * u + own, u), make_probs(own), tok, own + 1)
    run_items(n_items, lambda t: (tile_tab_ref[t], stage_tab_ref[t]), make_probs(None), tok, G)

    def finish(i, c):
        l = l_ref[i]
        acc = acc_ref[i]
        out_t = jnp.concatenate([acc[:DH] / l[:, :BLK], acc[DH:] / l[:, BLK:]], axis=0)
        o_ref[pl.ds(pl.multiple_of(i * BLK, BLK), BLK), :] = out_t.T.astype(BF16)
        return c

    lax.fori_loop(0, nb, finish, 0, unroll=4)


def _moba(mq, mk, mvt, *, batch, seq):
    BLK = MOBA_BLOCK
    nb = seq // BLK
    G = MOBA_STAGE
    assert nb % (4 * G) == 0
    npair = MOBA_W // LANES
    items = [(i, st) for st in range(nb // G) for i in range(G * st + G, nb)]
    assert len(items) % 4 == 0
    tile_tab = jnp.asarray([i for i, _ in items], jnp.int32)
    stage_tab = jnp.asarray([st for _, st in items], jnp.int32)
    qkspec = pl.BlockSpec((seq, LANES), lambda b, hp, *_: (b, hp))
    vtspec = pl.BlockSpec((1, 1, nb, LANES, BLK), lambda b, hp, *_: (b, hp, 0, 0, 0))
    stat = pltpu.VMEM((nb, 1, 2 * BLK), F32)
    sbuf = pltpu.VMEM((G * BLK, 2 * BLK), F32)
    pbuf = pltpu.VMEM((G * BLK, 2 * BLK), BF16)
    return pl.pallas_call(
        functools.partial(_moba_kernel, n_items=len(items)),
        grid_spec=pltpu.PrefetchScalarGridSpec(
            num_scalar_prefetch=2,
            grid=(batch, npair),
            in_specs=[qkspec, qkspec, vtspec],
            out_specs=qkspec,
            scratch_shapes=[pltpu.VMEM((nb, LANES), F32), pltpu.VMEM((nb, nb, 2 * BLK), F32),
                            pltpu.VMEM((nb, 2 * BLK, LANES), BF16),
                            stat, stat, stat, pltpu.VMEM((nb, LANES, BLK), F32),
                            sbuf, sbuf, sbuf, sbuf, pbuf, pbuf, pbuf, pbuf]),
        out_shape=jax.ShapeDtypeStruct((batch * seq, MOBA_W), BF16),
        compiler_params=pltpu.CompilerParams(dimension_semantics=("arbitrary",) * 2,
                                             vmem_limit_bytes=VMEM_LIMIT),
        name="moba",
    )(tile_tab, stage_tab, mq, mk, mvt)


def _merge_kernel(x_ref, ret_ref, moba_ref, ga_ref, gb_ref, wr_ref, wm_ref, wo_ref, o_ref):
    a = _dot(ret_ref[...], wr_ref[...])
    b = _dot(moba_ref[...], wm_ref[...])
    mix = ga_ref[...].astype(F32) * a + gb_ref[...].astype(F32) * b
    o_ref[...] = x_ref[...] + _dot(mix.astype(BF16), wo_ref[...])


def _merge(x2, ret, moba, sga, sgb, w_ret_out, w_moba_out, w_o, *, tm):
    T = x2.shape[0]
    row = lambda w: pl.BlockSpec((tm, w), lambda i: (i, 0))
    return pl.pallas_call(
        _merge_kernel,
        grid=(T // tm,),
        in_specs=[row(D_MODEL), row(RET_V), row(MOBA_W), row(D_MODEL), row(D_MODEL),
                  _const_spec((RET_V, D_MODEL)), _const_spec((MOBA_W, D_MODEL)), _const_spec((D_MODEL, D_MODEL))],
        out_specs=row(D_MODEL),
        out_shape=jax.ShapeDtypeStruct((T, D_MODEL), F32),
        compiler_params=pltpu.CompilerParams(dimension_semantics=("arbitrary",), vmem_limit_bytes=VMEM_LIMIT),
        name="merge",
    )(x2, ret, moba, sga, sgb, w_ret_out.astype(BF16), w_moba_out.astype(BF16), w_o.astype(BF16))


def _ffn_kernel(x_ref, n2_ref, wg_ref, wu_ref, wd_ref, o_ref):
    x = x_ref[...]
    ms = jnp.mean(x * x, axis=-1, keepdims=True)
    h = (x * lax.rsqrt(ms + RMS_EPS) * n2_ref[...]).astype(BF16)
    out = x
    for c0, c1 in FFN_COL_CHUNKS:
        g = _dot(h, wg_ref[:, c0:c1])
        u = _dot(h, wu_ref[:, c0:c1])
        act = (g * jax.nn.sigmoid(g) * u).astype(BF16)
        out = out + _dot(act, wd_ref[c0:c1, :])
    o_ref[...] = out


def _ffn(x2, norm2_w, w_gate, w_up, w_down, *, tm):
    T = x2.shape[0]
    row = pl.BlockSpec((tm, D_MODEL), lambda i: (i, 0))
    return pl.pallas_call(
        _ffn_kernel,
        grid=(T // tm,),
        in_specs=[row, _const_spec((1, D_MODEL)), _const_spec((D_MODEL, FFN_HIDDEN)),
                  _const_spec((D_MODEL, FFN_HIDDEN)), _const_spec((FFN_HIDDEN, D_MODEL))],
        out_specs=row,
        out_shape=jax.ShapeDtypeStruct((T, D_MODEL), F32),
        compiler_params=pltpu.CompilerParams(dimension_semantics=("arbitrary",), vmem_limit_bytes=VMEM_LIMIT),
        name="ffn",
    )(x2, norm2_w.astype(F32)[None, :], w_gate.astype(BF16), w_up.astype(BF16), w_down.astype(BF16))


def kernel(x, norm1_w, w_in, q_norm_w, k_norm_w, w_ret_out, w_moba_out, w_o, norm2_w, w_ffn_gate, w_ffn_up, w_ffn_down):
    B, S, D = x.shape
    assert D == D_MODEL and S % MOBA_BLOCK == 0 and S % MERGE_ROW_TILE == 0 and S % ROW_TILE == 0
    depth = norm1_w.shape[0]
    x2 = x.reshape(B * S, D)
    for l in range(depth):
        rq, rk, rv, rgs, mq, mk, mvt, sga, sgb = _inproj(
            x2, norm1_w[l], w_in[l], q_norm_w[l], k_norm_w[l], seq=S, tm=ROW_TILE)
        ret = _retention(rq, rk, rv, rgs, batch=B, seq=S, rt=ROW_TILE)
        moba = _moba(mq, mk, mvt, batch=B, seq=S)
        x2 = _merge(x2, ret, moba, sga, sgb, w_ret_out[l], w_moba_out[l], w_o[l], tm=MERGE_ROW_TILE)
        x2 = _ffn(x2, norm2_w[l], w_ffn_gate[l], w_ffn_up[l], w_ffn_down[l], tm=FFN_ROW_TILE)
    return x2.reshape(B, S, D)
```

```python
import functools
import math

import numpy as np
import jax
import jax.numpy as jnp
from jax import lax
from jax.experimental import pallas as pl
from jax.experimental.pallas import tpu as pltpu

D_MODEL = 1024
RET_HEADS = 4
RET_DK = 128
RET_DV = 256
ROPE_BASE = 10000.0
MOBA_HEADS = 8
MOBA_DH = 64
MOBA_BLOCK = 256
MOBA_TOPK = 3
FFN_HIDDEN = 2816
RET_QK = RET_HEADS * RET_DK
RET_V = RET_HEADS * RET_DV
MOBA_W = MOBA_HEADS * MOBA_DH
IN_SPLITS = (RET_QK, RET_QK, RET_V, RET_V, MOBA_W, MOBA_W, MOBA_W, D_MODEL, D_MODEL)
IN_COLS = sum(IN_SPLITS)
RMS_EPS = 1e-6
GN_EPS = 1e-5
NEG = -1e30

LANES = 128
BF16_ROWS = 16
VMEM_LIMIT = 52 * 1024 * 1024
RET_CHUNK = 256
MOBA_STAGE = 4
ROW_TILE = 512
MERGE_ROW_TILE = 1024
FFN_ROW_TILE = 1024
FFN_COL_CHUNKS = ((0, 1536), (1536, FFN_HIDDEN))

F32 = jnp.float32
BF16 = jnp.bfloat16


def _dot(a, b):
    return jnp.dot(a, b, preferred_element_type=F32)


def _dot_nt(a, b):
    return lax.dot_general(a, b, (((1,), (1,)), ((), ())), preferred_element_type=F32)


def _dot_tn(a, b):
    return lax.dot_general(a, b, (((0,), (0,)), ((), ())), preferred_element_type=F32)


def _const_spec(shape):
    nd = len(shape)
    return pl.BlockSpec(shape, lambda *_: (0,) * nd, pipeline_mode=pl.Buffered(1))


def _inproj_kernel(x_ref, xn_ref, n1_ref, w_ref, qn_ref, kn_ref, cos_ref, sin_ref,
                   rq_ref, rk_ref, rv_ref, rg_ref, mq_ref, mk_ref, mvt_ref, ga_ref, gb_ref, h_ref,
                   *, tm):
    i = pl.program_id(0)

    def normed(x):
        ms = jnp.mean(x * x, axis=-1, keepdims=True)
        return (x * lax.rsqrt(ms + RMS_EPS) * n1_ref[...]).astype(BF16)

    @pl.when(i == 0)
    def _():
        h_ref[0] = normed(x_ref[...])

    h = h_ref[i % 2]

    offs = np.cumsum((0,) + IN_SPLITS)

    def proj(n):
        return _dot(h, w_ref[:, int(offs[n]):int(offs[n + 1])])

    cos = cos_ref[...]
    sin = sin_ref[...]
    lane = lax.broadcasted_iota(jnp.int32, (tm, LANES), 1)

    def rotary_store(p, out_ref, scale):
        for hd in range(RET_HEADS):
            xh = p[:, hd * RET_DK:(hd + 1) * RET_DK]
            y = xh * cos + pltpu.roll(xh, RET_DK // 2, 1) * sin
            if scale is not None:
                y = y * scale
            out_ref[:, hd * RET_DK:(hd + 1) * RET_DK] = y.astype(BF16)

    rotary_store(proj(0), rq_ref, None)
    rotary_store(proj(1), rk_ref, RET_DK ** -0.5)
    rv_ref[...] = proj(2).astype(BF16)
    rg = proj(3)
    rg_ref[...] = (rg * jax.nn.sigmoid(rg)).astype(BF16)

    low_half = lane < MOBA_DH

    def head_rms_store(p, w_row_ref, out_ref, scale):
        for g in range(MOBA_W // LANES):
            cols = slice(g * LANES, (g + 1) * LANES)
            ph = p[:, cols]
            sq = ph * ph
            lo = jnp.sum(jnp.where(low_half, sq, 0.0), axis=-1, keepdims=True)
            hi = jnp.sum(jnp.where(low_half, 0.0, sq), axis=-1, keepdims=True)
            msq = jnp.where(low_half, lo, hi) * (1.0 / MOBA_DH)
            y = ph * lax.rsqrt(msq + RMS_EPS) * w_row_ref[:, cols]
            out_ref[:, cols] = (y if scale is None else y * scale).astype(BF16)

    head_rms_store(proj(4), qn_ref, mq_ref, MOBA_DH ** -0.5 * math.log2(math.e))
    head_rms_store(proj(5), kn_ref, mk_ref, None)
    mvt = proj(6).T.astype(BF16)
    for pair in range(MOBA_W // LANES):
        for c in range(tm // MOBA_BLOCK):
            mvt_ref[0, pair, c] = mvt[pair * LANES:(pair + 1) * LANES, c * MOBA_BLOCK:(c + 1) * MOBA_BLOCK]
    ga_ref[...] = jax.nn.sigmoid(proj(7)).astype(BF16)
    gb_ref[...] = jax.nn.sigmoid(proj(8)).astype(BF16)
    h_ref[(i + 1) % 2] = normed(xn_ref[...])


def _inproj(x2, norm1_w, w_in, q_norm_w, k_norm_w, *, seq, tm):
    T = x2.shape[0]
    half = RET_DK // 2
    inv = ROPE_BASE ** (-np.arange(half, dtype=np.float64) / half)
    ang = np.arange(seq, dtype=np.float64)[:, None] * inv[None, :]
    cos2 = jnp.asarray(np.concatenate([np.cos(ang), np.cos(ang)], axis=1), F32)
    sin2 = jnp.asarray(np.concatenate([-np.sin(ang), np.sin(ang)], axis=1), F32)
    qn = jnp.tile(q_norm_w.astype(F32), MOBA_HEADS)[None, :]
    kn = jnp.tile(k_norm_w.astype(F32), MOBA_HEADS)[None, :]
    offs = np.cumsum((0,) + IN_SPLITS)
    n_tiles = T // tm
    nt = seq // tm
    npair = MOBA_W // LANES
    nb = seq // MOBA_BLOCK
    row = lambda w: pl.BlockSpec((tm, w), lambda i: (i, 0))
    next_row = pl.BlockSpec((tm, D_MODEL), lambda i: (jnp.minimum(i + 1, n_tiles - 1), 0))
    pos_row = pl.BlockSpec((tm, LANES), lambda i: (i % nt, 0))
    row_out = lambda w: (row(w), jax.ShapeDtypeStruct((T, w), BF16))
    mvt_out = (pl.BlockSpec((1, npair, tm // MOBA_BLOCK, LANES, MOBA_BLOCK), lambda i: (i // nt, 0, i % nt, 0, 0)),
               jax.ShapeDtypeStruct((T // seq, npair, nb, LANES, MOBA_BLOCK), BF16))
    outs = [row_out(RET_QK), row_out(RET_QK), row_out(RET_V), row_out(RET_V), row_out(MOBA_W), row_out(MOBA_W),
            mvt_out, row_out(D_MODEL), row_out(D_MODEL)]
    return pl.pallas_call(
        functools.partial(_inproj_kernel, tm=tm),
        grid=(n_tiles,),
        in_specs=[row(D_MODEL), next_row, _const_spec((1, D_MODEL)), _const_spec((D_MODEL, IN_COLS)),
                  _const_spec((1, MOBA_W)), _const_spec((1, MOBA_W)), pos_row, pos_row],
        out_specs=[o[0] for o in outs],
        out_shape=[o[1] for o in outs],
        scratch_shapes=[pltpu.VMEM((2, tm, D_MODEL), BF16)],
        compiler_params=pltpu.CompilerParams(dimension_semantics=("arbitrary",), vmem_limit_bytes=VMEM_LIMIT),
        name="inproj",
    )(x2, x2, norm1_w.astype(F32)[None, :], w_in.astype(BF16), qn, kn, cos2, sin2)


def _retention_kernel(q_ref, k_ref, v_ref, g_ref, dmat_ref, xi_ref, zeta_ref, o_ref, state_ref,
                      *, rt, chunk, g_chunk):
    @pl.when(pl.program_id(1) == 0)
    def _():
        state_ref[...] = jnp.zeros_like(state_ref)

    for c in range(rt // chunk):
        rows = slice(c * chunk, (c + 1) * chunk)
        for hd in range(RET_HEADS):
            kcols = slice(hd * RET_DK, (hd + 1) * RET_DK)
            vcols = slice(hd * RET_DV, (hd + 1) * RET_DV)
            q = q_ref[rows, kcols]
            k = k_ref[rows, kcols]
            v = v_ref[rows, vcols]
            state = state_ref[hd]
            scores = _dot_nt(q, k) * dmat_ref[hd]
            q_dec = (q.astype(F32) * xi_ref[hd]).astype(BF16)
            o = _dot(scores.astype(BF16), v) + _dot(q_dec, state.astype(BF16))
            k_dec = (k.astype(F32) * zeta_ref[hd]).astype(BF16)
            state_ref[hd] = g_chunk[hd] * state + _dot_tn(k_dec, v)
            mu = jnp.mean(o, axis=-1, keepdims=True)
            d = o - mu
            var = jnp.mean(d * d, axis=-1, keepdims=True)
            y = d * lax.rsqrt(var + GN_EPS)
            o_ref[rows, vcols] = (g_ref[rows, vcols].astype(F32) * y).astype(BF16)


def _retention(rq, rk, rv, rgs, *, batch, seq, rt):
    C = RET_CHUNK
    lg = np.log1p(-np.exp2(-5.0 - np.arange(RET_HEADS, dtype=np.float64)))
    idx = np.arange(C, dtype=np.float64)
    diff = idx[:, None] - idx[None, :]
    dmat = np.where(diff >= 0, np.exp(np.maximum(diff, 0.0)[None] * lg[:, None, None]), 0.0)
    xi = np.broadcast_to(np.exp((idx + 1.0)[None, :] * lg[:, None])[:, :, None], (RET_HEADS, C, LANES))
    zeta = np.broadcast_to(np.exp((C - 1.0 - idx)[None, :] * lg[:, None])[:, :, None], (RET_HEADS, C, LANES))
    g_chunk = tuple(float(v) for v in np.exp(C * lg))
    nt = seq // rt
    row = lambda w: pl.BlockSpec((rt, w), lambda b, j: (b * nt + j, 0))
    return pl.pallas_call(
        functools.partial(_retention_kernel, rt=rt, chunk=C, g_chunk=g_chunk),
        grid=(batch, nt),
        in_specs=[row(RET_QK), row(RET_QK), row(RET_V), row(RET_V),
                  _const_spec((RET_HEADS, C, C)), _const_spec((RET_HEADS, C, LANES)),
                  _const_spec((RET_HEADS, C, LANES))],
        out_specs=row(RET_V),
        out_shape=jax.ShapeDtypeStruct((batch * seq, RET_V), BF16),
        scratch_shapes=[pltpu.VMEM((RET_HEADS, RET_DK, RET_DV), F32)],
        compiler_params=pltpu.CompilerParams(dimension_semantics=("arbitrary", "arbitrary"),
                                             vmem_limit_bytes=VMEM_LIMIT),
        name="retention",
    )(rq, rk, rv, rgs, jnp.asarray(dmat, F32), jnp.asarray(xi, F32), jnp.asarray(zeta, F32))


def _moba_kernel(tile_tab_ref, stage_tab_ref, q_ref, k_ref, vt_ref, o_ref,
                 kbar_ref, bias_ref, qcat_ref, m_ref, r_ref, l_ref, acc_ref,
                 s0_ref, s1_ref, s2_ref, s3_ref, p0_ref, p1_ref, p2_ref, p3_ref, *, n_items):
    BLK = MOBA_BLOCK
    DH = MOBA_DH
    G = MOBA_STAGE
    STG = G * BLK
    nb = vt_ref.shape[2]
    nbp = kbar_ref.shape[0]
    s_bufs = (s0_ref, s1_ref, s2_ref, s3_ref)
    p_bufs = (p0_ref, p1_ref, p2_ref, p3_ref)
    lane = lax.broadcasted_iota(jnp.int32, (1, LANES), 1)

    def q_cat_body(i, c):
        q_pair = q_ref[pl.ds(pl.multiple_of(i * BLK, BLK), BLK), :]
        zero = jnp.zeros_like(q_pair)
        qcat_ref[i, :BLK, :] = jnp.where(lane < DH, q_pair, zero)
        qcat_ref[i, BLK:, :] = jnp.where(lane >= DH, q_pair, zero)
        return c

    lax.fori_loop(0, nb, q_cat_body, 0, unroll=4)

    def q_cat_of(i):
        return qcat_ref[i]

    kbar_ref[...] = jnp.zeros_like(kbar_ref)

    def kbar_body(n, c):
        kb = k_ref[pl.ds(pl.multiple_of(n * BLK, BLK), BLK), :]
        kbar_ref[pl.ds(n, 1), :] = jnp.mean(kb.astype(F32), axis=0, keepdims=True)
        return c

    lax.fori_loop(0, nb, kbar_body, 0, unroll=4)

    kbar = kbar_ref[...]
    kbar_hi = kbar.astype(BF16)
    kbar_lo = (kbar - kbar_hi.astype(F32)).astype(BF16)
    blk_idx = lax.broadcasted_iota(jnp.int32, (nbp, 2 * BLK), 0)

    def select_blocks(i):
        q_cat = q_cat_of(i)
        gate = _dot_nt(kbar_hi, q_cat) + _dot_nt(kbar_lo, q_cat)
        past = blk_idx < i
        gate = jnp.where(past, gate, NEG)
        sel = jnp.zeros((nbp, 2 * BLK), jnp.bool_)
        for _ in range(MOBA_TOPK):
            best = jnp.max(gate, axis=0, keepdims=True)
            first = jnp.min(jnp.where(gate == best, blk_idx, nbp), axis=0, keepdims=True)
            pick = blk_idx == first
            sel = sel | (pick & past)
            gate = jnp.where(pick, -3e38, gate)
        bias_ref[i] = jnp.where(sel, 0.0, NEG)

    def select_body(u, c):
        select_blocks(2 * u)
        select_blocks(2 * u + 1)
        return c

    lax.fori_loop(0, nb // 2, select_body, 0, unroll=4)

    m_ref[...] = jnp.full(m_ref.shape, NEG, F32)
    r_ref[...] = jnp.full(r_ref.shape, NEG, F32)
    l_ref[...] = jnp.zeros_like(l_ref)
    acc_ref[...] = jnp.zeros_like(acc_ref)

    def scores(i, st, dst_ref, n_live):
        rows = n_live * BLK
        ks = k_ref[pl.ds(pl.multiple_of(st * STG, STG), rows), :]
        dst_ref[:rows, :] = _dot_nt(ks, q_cat_of(i))

    def values(st, p_ref, n_live):
        rows = n_live * BLK
        vt = jnp.concatenate([vt_ref[0, 0, G * st + g] for g in range(n_live)], axis=1)
        pv = _dot(jnp.concatenate([vt, jnp.ones((BF16_ROWS, rows), BF16)], axis=0), p_ref[:rows, :])
        return pv[:DH, :BLK], pv[DH:2 * DH, BLK:], pv[2 * DH:2 * DH + 1, :]

    def fold(pv, m_pv, i):
        num0, num1, sums = pv
        scale = jnp.exp2(r_ref[i] - m_pv)
        r_ref[i] = m_pv
        l_ref[i] = scale * l_ref[i] + sums
        acc_ref[i, :DH, :] = scale[:, :BLK] * acc_ref[i, :DH, :] + num0
        acc_ref[i, DH:, :] = scale[:, BLK:] * acc_ref[i, DH:, :] + num1

    key_idx = lax.broadcasted_iota(jnp.int32, (BLK, 2 * BLK), 0)
    qry_idx = lax.broadcasted_iota(jnp.int32, (BLK, 2 * BLK), 1) & (BLK - 1)
    causal = key_idx <= qry_idx

    def make_probs(own):
        def probs(i, st, s_ref, p_ref, tok):
            n_live = G if own is None else own + 1
            tiles, shifts = [], []
            m_new = m_ref[i]
            for g in range(n_live):
                sg = s_ref[g * BLK:(g + 1) * BLK, :]
                if g == own:
                    sg = jnp.where(causal, sg, NEG)
                    bg = None
                    m_new = jnp.maximum(m_new, jnp.max(sg, axis=0, keepdims=True))
                else:
                    bg = bias_ref[i, pl.ds(G * st + g, 1), :]
                    m_new = jnp.maximum(m_new, jnp.max(sg, axis=0, keepdims=True) + bg)
                tiles.append(sg)
                shifts.append(bg)
            m_ref[i] = m_new
            m_new = m_new + tok
            for g in range(n_live):
                shift = m_new if shifts[g] is None else m_new - shifts[g]
                p_ref[g * BLK:(g + 1) * BLK, :] = jnp.exp2(tiles[g] - shift).astype(BF16)
            return m_new
        return probs

    def run_items(count, item, probs, tok0, n_live):
        def clamped(t):
            return item(jnp.clip(t, 0, count - 1))

        scores(*clamped(0), s_bufs[0], n_live)
        scores(*clamped(1), s_bufs[1], n_live)

        def slot(t, k, carry, look_ahead):
            m_prev, toks = carry
            if look_ahead:
                scores(*clamped(t + 2), s_bufs[(k + 2) % 4], n_live)
            i_prev, st_prev = clamped(t - 1)
            pv = values(st_prev, p_bufs[(k - 1) % 4], n_live)
            i, st = clamped(t)
            m_t = probs(i, st, s_bufs[k % 4], p_bufs[k % 4], toks[0])
            fold(pv, m_prev, i_prev)
            return m_t, (toks[1], toks[2], pv[2] * 0.0)

        unroll = 4
        trips = count // unroll

        def body(it, carry, last=False):
            for k in range(unroll):
                carry = slot(unroll * it + k, k, carry, look_ahead=not (last and k + 2 >= unroll))
            return carry

        carry = lax.fori_loop(0, trips - 1, body, (r_ref[clamped(0)[0]], (tok0, tok0, tok0)))
        m_last, _ = body(trips - 1, carry, last=True)
        i_last, st_last = clamped(count - 1)
        pv_last = values(st_last, p_bufs[(count - 1) % 4], n_live)
        fold(pv_last, m_last, i_last)
        tok_last = pv_last[2] * 0.0
        p_bufs[3][...] = jnp.broadcast_to(tok_last.astype(BF16), p_bufs[3].shape)
        return tok_last

    p_bufs[3][...] = jnp.zeros_like(p_bufs[3])
    tok = jnp.zeros((1, 2 * BLK), F32)
    for own in range(G):
        tok = run_items(nb // G, lambda u, own=own: (G * u + own, u), make_probs(own), tok, own + 1)
    run_items(n_items, lambda t: (tile_tab_ref[t], stage_tab_ref[t]), make_probs(None), tok, G)

    def finish(i, c):
        l = l_ref[i]
        acc = acc_ref[i]
        out_t = jnp.concatenate([acc[:DH] / l[:, :BLK], acc[DH:] / l[:, BLK:]], axis=0)
        o_ref[pl.ds(pl.multiple_of(i * BLK, BLK), BLK), :] = out_t.T.astype(BF16)
        return c

    lax.fori_loop(0, nb, finish, 0, unroll=8)


def _moba(mq, mk, mvt, *, batch, seq):
    BLK = MOBA_BLOCK
    nb = seq // BLK
    G = MOBA_STAGE
    assert nb % (4 * G) == 0
    npair = MOBA_W // LANES
    items = [(i, st) for st in range(nb // G) for i in range(G * st + G, nb)]
    assert len(items) % 4 == 0
    tile_tab = jnp.asarray([i for i, _ in items], jnp.int32)
    stage_tab = jnp.asarray([st for _, st in items], jnp.int32)
    qkspec = pl.BlockSpec((seq, LANES), lambda b, hp, *_: (b, hp))
    vtspec = pl.BlockSpec((1, 1, nb, LANES, BLK), lambda b, hp, *_: (b, hp, 0, 0, 0))
    stat = pltpu.VMEM((nb, 1, 2 * BLK), F32)
    sbuf = pltpu.VMEM((G * BLK, 2 * BLK), F32)
    pbuf = pltpu.VMEM((G * BLK, 2 * BLK), BF16)
    return pl.pallas_call(
        functools.partial(_moba_kernel, n_items=len(items)),
        grid_spec=pltpu.PrefetchScalarGridSpec(
            num_scalar_prefetch=2,
            grid=(batch, npair),
            in_specs=[qkspec, qkspec, vtspec],
            out_specs=qkspec,
            scratch_shapes=[pltpu.VMEM((nb, LANES), F32), pltpu.VMEM((nb, nb, 2 * BLK), F32),
                            pltpu.VMEM((nb, 2 * BLK, LANES), BF16),
                            stat, stat, stat, pltpu.VMEM((nb, LANES, BLK), F32),
                            sbuf, sbuf, sbuf, sbuf, pbuf, pbuf, pbuf, pbuf]),
        out_shape=jax.ShapeDtypeStruct((batch * seq, MOBA_W), BF16),
        compiler_params=pltpu.CompilerParams(dimension_semantics=("arbitrary",) * 2,
                                             vmem_limit_bytes=VMEM_LIMIT),
        name="moba",
    )(tile_tab, stage_tab, mq, mk, mvt)


def _merge_kernel(x_ref, ret_ref, moba_ref, ga_ref, gb_ref, wr_ref, wm_ref, wo_ref, o_ref):
    a = _dot(ret_ref[...], wr_ref[...])
    b = _dot(moba_ref[...], wm_ref[...])
    mix = ga_ref[...].astype(F32) * a + gb_ref[...].astype(F32) * b
    o_ref[...] = x_ref[...] + _dot(mix.astype(BF16), wo_ref[...])


def _merge(x2, ret, moba, sga, sgb, w_ret_out, w_moba_out, w_o, *, tm):
    T = x2.shape[0]
    row = lambda w: pl.BlockSpec((tm, w), lambda i: (i, 0))
    return pl.pallas_call(
        _merge_kernel,
        grid=(T // tm,),
        in_specs=[row(D_MODEL), row(RET_V), row(MOBA_W), row(D_MODEL), row(D_MODEL),
                  _const_spec((RET_V, D_MODEL)), _const_spec((MOBA_W, D_MODEL)), _const_spec((D_MODEL, D_MODEL))],
        out_specs=row(D_MODEL),
        out_shape=jax.ShapeDtypeStruct((T, D_MODEL), F32),
        compiler_params=pltpu.CompilerParams(dimension_semantics=("arbitrary",), vmem_limit_bytes=VMEM_LIMIT),
        name="merge",
    )(x2, ret, moba, sga, sgb, w_ret_out.astype(BF16), w_moba_out.astype(BF16), w_o.astype(BF16))


def _ffn_kernel(x_ref, n2_ref, wg_ref, wu_ref, wd_ref, o_ref):
    x = x_ref[...]
    ms = jnp.mean(x * x, axis=-1, keepdims=True)
    h = (x * lax.rsqrt(ms + RMS_EPS) * n2_ref[...]).astype(BF16)
    out = x
    for c0, c1 in FFN_COL_CHUNKS:
        g = _dot(h, wg_ref[:, c0:c1])
        u = _dot(h, wu_ref[:, c0:c1])
        act = (g * jax.nn.sigmoid(g) * u).astype(BF16)
        out = out + _dot(act, wd_ref[c0:c1, :])
    o_ref[...] = out


def _ffn(x2, norm2_w, w_gate, w_up, w_down, *, tm):
    T = x2.shape[0]
    row = pl.BlockSpec((tm, D_MODEL), lambda i: (i, 0))
    return pl.pallas_call(
        _ffn_kernel,
        grid=(T // tm,),
        in_specs=[row, _const_spec((1, D_MODEL)), _const_spec((D_MODEL, FFN_HIDDEN)),
                  _const_spec((D_MODEL, FFN_HIDDEN)), _const_spec((FFN_HIDDEN, D_MODEL))],
        out_specs=row,
        out_shape=jax.ShapeDtypeStruct((T, D_MODEL), F32),
        compiler_params=pltpu.CompilerParams(dimension_semantics=("arbitrary",), vmem_limit_bytes=VMEM_LIMIT),
        name="ffn",
    )(x2, norm2_w.astype(F32)[None, :], w_gate.astype(BF16), w_up.astype(BF16), w_down.astype(BF16))


def kernel(x, norm1_w, w_in, q_norm_w, k_norm_w, w_ret_out, w_moba_out, w_o, norm2_w, w_ffn_gate, w_ffn_up, w_ffn_down):
    B, S, D = x.shape
    assert D == D_MODEL and S % MOBA_BLOCK == 0 and S % MERGE_ROW_TILE == 0 and S % ROW_TILE == 0
    depth = norm1_w.shape[0]
    x2 = x.reshape(B * S, D)
    for l in range(depth):
        rq, rk, rv, rgs, mq, mk, mvt, sga, sgb = _inproj(
            x2, norm1_w[l], w_in[l], q_norm_w[l], k_norm_w[l], seq=S, tm=ROW_TILE)
        ret = _retention(rq, rk, rv, rgs, batch=B, seq=S, rt=ROW_TILE)
        moba = _moba(mq, mk, mvt, batch=B, seq=S)
        x2 = _merge(x2, ret, moba, sga, sgb, w_ret_out[l], w_moba_out[l], w_o[l], tm=MERGE_ROW_TILE)
        x2 = _ffn(x2, norm2_w[l], w_ffn_gate[l], w_ffn_up[l], w_ffn_down[l], tm=FFN_ROW_TILE)
    return x2.reshape(B, S, D)
```

```python
import functools
import math

import numpy as np
import jax
import jax.numpy as jnp
from jax import lax
from jax.experimental import pallas as pl
from jax.experimental.pallas import tpu as pltpu

D_MODEL = 1024
RET_HEADS = 4
RET_DK = 128
RET_DV = 256
ROPE_BASE = 10000.0
MOBA_HEADS = 8
MOBA_DH = 64
MOBA_BLOCK = 256
MOBA_TOPK = 3
FFN_HIDDEN = 2816
RET_QK = RET_HEADS * RET_DK
RET_V = RET_HEADS * RET_DV
MOBA_W = MOBA_HEADS * MOBA_DH
IN_SPLITS = (RET_QK, RET_QK, RET_V, RET_V, MOBA_W, MOBA_W, MOBA_W, D_MODEL, D_MODEL)
IN_COLS = sum(IN_SPLITS)
RMS_EPS = 1e-6
GN_EPS = 1e-5
NEG = -1e30

LANES = 128
BF16_ROWS = 16
VMEM_LIMIT = 52 * 1024 * 1024
RET_CHUNK = 256
MOBA_STAGE = 4
ROW_TILE = 512
MERGE_ROW_TILE = 1024
FFN_ROW_TILE = 1024
FFN_COL_CHUNKS = ((0, 1536), (1536, FFN_HIDDEN))

F32 = jnp.float32
BF16 = jnp.bfloat16


def _dot(a, b):
    return jnp.dot(a, b, preferred_element_type=F32)


def _dot_nt(a, b):
    return lax.dot_general(a, b, (((1,), (1,)), ((), ())), preferred_element_type=F32)


def _dot_tn(a, b):
    return lax.dot_general(a, b, (((0,), (0,)), ((), ())), preferred_element_type=F32)


def _const_spec(shape):
    nd = len(shape)
    return pl.BlockSpec(shape, lambda *_: (0,) * nd, pipeline_mode=pl.Buffered(1))


def _inproj_kernel(x_ref, xn_ref, n1_ref, w_ref, qn_ref, kn_ref, cos_ref, sin_ref,
                   rq_ref, rk_ref, rv_ref, rg_ref, mq_ref, mk_ref, mvt_ref, ga_ref, gb_ref, h_ref,
                   *, tm):
    i = pl.program_id(0)

    def normed(x):
        ms = jnp.mean(x * x, axis=-1, keepdims=True)
        return (x * lax.rsqrt(ms + RMS_EPS) * n1_ref[...]).astype(BF16)

    @pl.when(i == 0)
    def _():
        h_ref[0] = normed(x_ref[...])

    h = h_ref[i % 2]

    offs = np.cumsum((0,) + IN_SPLITS)

    def proj(n):
        return _dot(h, w_ref[:, int(offs[n]):int(offs[n + 1])])

    cos = cos_ref[...]
    sin = sin_ref[...]
    lane = lax.broadcasted_iota(jnp.int32, (tm, LANES), 1)

    def rotary_store(p, out_ref, scale):
        for hd in range(RET_HEADS):
            xh = p[:, hd * RET_DK:(hd + 1) * RET_DK]
            y = xh * cos + pltpu.roll(xh, RET_DK // 2, 1) * sin
            if scale is not None:
                y = y * scale
            out_ref[:, hd * RET_DK:(hd + 1) * RET_DK] = y.astype(BF16)

    rotary_store(proj(0), rq_ref, None)
    rotary_store(proj(1), rk_ref, RET_DK ** -0.5)
    rv_ref[...] = proj(2).astype(BF16)
    rg = proj(3)
    rg_ref[...] = (rg * jax.nn.sigmoid(rg)).astype(BF16)

    low_half = lane < MOBA_DH

    def head_rms_store(p, w_row_ref, out_ref, scale):
        for g in range(MOBA_W // LANES):
            cols = slice(g * LANES, (g + 1) * LANES)
            ph = p[:, cols]
            sq = ph * ph
            lo = jnp.sum(jnp.where(low_half, sq, 0.0), axis=-1, keepdims=True)
            hi = jnp.sum(jnp.where(low_half, 0.0, sq), axis=-1, keepdims=True)
            msq = jnp.where(low_half, lo, hi) * (1.0 / MOBA_DH)
            y = ph * lax.rsqrt(msq + RMS_EPS) * w_row_ref[:, cols]
            out_ref[:, cols] = (y if scale is None else y * scale).astype(BF16)

    head_rms_store(proj(4), qn_ref, mq_ref, MOBA_DH ** -0.5 * math.log2(math.e))
    head_rms_store(proj(5), kn_ref, mk_ref, None)
    mvt = proj(6).T.astype(BF16)
    for pair in range(MOBA_W // LANES):
        for c in range(tm // MOBA_BLOCK):
            mvt_ref[0, pair, c] = mvt[pair * LANES:(pair + 1) * LANES, c * MOBA_BLOCK:(c + 1) * MOBA_BLOCK]
    ga_ref[...] = jax.nn.sigmoid(proj(7)).astype(BF16)
    gb_ref[...] = jax.nn.sigmoid(proj(8)).astype(BF16)
    h_ref[(i + 1) % 2] = normed(xn_ref[...])


def _inproj(x2, norm1_w, w_in, q_norm_w, k_norm_w, *, seq, tm):
    T = x2.shape[0]
    half = RET_DK // 2
    inv = ROPE_BASE ** (-np.arange(half, dtype=np.float64) / half)
    ang = np.arange(seq, dtype=np.float64)[:, None] * inv[None, :]
    cos2 = jnp.asarray(np.concatenate([np.cos(ang), np.cos(ang)], axis=1), F32)
    sin2 = jnp.asarray(np.concatenate([-np.sin(ang), np.sin(ang)], axis=1), F32)
    qn = jnp.tile(q_norm_w.astype(F32), MOBA_HEADS)[None, :]
    kn = jnp.tile(k_norm_w.astype(F32), MOBA_HEADS)[None, :]
    offs = np.cumsum((0,) + IN_SPLITS)
    n_tiles = T // tm
    nt = seq // tm
    npair = MOBA_W // LANES
    nb = seq // MOBA_BLOCK
    row = lambda w: pl.BlockSpec((tm, w), lambda i: (i, 0))
    next_row = pl.BlockSpec((tm, D_MODEL), lambda i: (jnp.minimum(i + 1, n_tiles - 1), 0))
    pos_row = pl.BlockSpec((tm, LANES), lambda i: (i % nt, 0))
    row_out = lambda w: (row(w), jax.ShapeDtypeStruct((T, w), BF16))
    mvt_out = (pl.BlockSpec((1, npair, tm // MOBA_BLOCK, LANES, MOBA_BLOCK), lambda i: (i // nt, 0, i % nt, 0, 0)),
               jax.ShapeDtypeStruct((T // seq, npair, nb, LANES, MOBA_BLOCK), BF16))
    outs = [row_out(RET_QK), row_out(RET_QK), row_out(RET_V), row_out(RET_V), row_out(MOBA_W), row_out(MOBA_W),
            mvt_out, row_out(D_MODEL), row_out(D_MODEL)]
    return pl.pallas_call(
        functools.partial(_inproj_kernel, tm=tm),
        grid=(n_tiles,),
        in_specs=[row(D_MODEL), next_row, _const_spec((1, D_MODEL)), _const_spec((D_MODEL, IN_COLS)),
                  _const_spec((1, MOBA_W)), _const_spec((1, MOBA_W)), pos_row, pos_row],
        out_specs=[o[0] for o in outs],
        out_shape=[o[1] for o in outs],
        scratch_shapes=[pltpu.VMEM((2, tm, D_MODEL), BF16)],
        compiler_params=pltpu.CompilerParams(dimension_semantics=("arbitrary",), vmem_limit_bytes=VMEM_LIMIT),
        name="inproj",
    )(x2, x2, norm1_w.astype(F32)[None, :], w_in.astype(BF16), qn, kn, cos2, sin2)


def _retention_kernel(q_ref, k_ref, v_ref, g_ref, dmat_ref, xi_ref, zeta_ref, o_ref, state_ref,
                      *, rt, chunk, g_chunk):
    @pl.when(pl.program_id(1) == 0)
    def _():
        state_ref[...] = jnp.zeros_like(state_ref)

    for c in range(rt // chunk):
        rows = slice(c * chunk, (c + 1) * chunk)
        for hd in range(RET_HEADS):
            kcols = slice(hd * RET_DK, (hd + 1) * RET_DK)
            vcols = slice(hd * RET_DV, (hd + 1) * RET_DV)
            q = q_ref[rows, kcols]
            k = k_ref[rows, kcols]
            v = v_ref[rows, vcols]
            state = state_ref[hd]
            scores = _dot_nt(q, k) * dmat_ref[hd]
            q_dec = (q.astype(F32) * xi_ref[hd]).astype(BF16)
            o = _dot(scores.astype(BF16), v) + _dot(q_dec, state.astype(BF16))
            k_dec = (k.astype(F32) * zeta_ref[hd]).astype(BF16)
            state_ref[hd] = g_chunk[hd] * state + _dot_tn(k_dec, v)
            mu = jnp.mean(o, axis=-1, keepdims=True)
            d = o - mu
            var = jnp.mean(d * d, axis=-1, keepdims=True)
            y = d * lax.rsqrt(var + GN_EPS)
            o_ref[rows, vcols] = (g_ref[rows, vcols].astype(F32) * y).astype(BF16)


def _retention(rq, rk, rv, rgs, *, batch, seq, rt):
    C = RET_CHUNK
    lg = np.log1p(-np.exp2(-5.0 - np.arange(RET_HEADS, dtype=np.float64)))
    idx = np.arange(C, dtype=np.float64)
    diff = idx[:, None] - idx[None, :]
    dmat = np.where(diff >= 0, np.exp(np.maximum(diff, 0.0)[None] * lg[:, None, None]), 0.0)
    xi = np.broadcast_to(np.exp((idx + 1.0)[None, :] * lg[:, None])[:, :, None], (RET_HEADS, C, LANES))
    zeta = np.broadcast_to(np.exp((C - 1.0 - idx)[None, :] * lg[:, None])[:, :, None], (RET_HEADS, C, LANES))
    g_chunk = tuple(float(v) for v in np.exp(C * lg))
    nt = seq // rt
    row = lambda w: pl.BlockSpec((rt, w), lambda b, j: (b * nt + j, 0))
    return pl.pallas_call(
        functools.partial(_retention_kernel, rt=rt, chunk=C, g_chunk=g_chunk),
        grid=(batch, nt),
        in_specs=[row(RET_QK), row(RET_QK), row(RET_V), row(RET_V),
                  _const_spec((RET_HEADS, C, C)), _const_spec((RET_HEADS, C, LANES)),
                  _const_spec((RET_HEADS, C, LANES))],
        out_specs=row(RET_V),
        out_shape=jax.ShapeDtypeStruct((batch * seq, RET_V), BF16),
        scratch_shapes=[pltpu.VMEM((RET_HEADS, RET_DK, RET_DV), F32)],
        compiler_params=pltpu.CompilerParams(dimension_semantics=("arbitrary", "arbitrary"),
                                             vmem_limit_bytes=VMEM_LIMIT),
        name="retention",
    )(rq, rk, rv, rgs, jnp.asarray(dmat, F32), jnp.asarray(xi, F32), jnp.asarray(zeta, F32))


def _moba_kernel(tile_tab_ref, stage_tab_ref, q_ref, k_ref, vt_ref, o_ref,
                 kbar_ref, bias_ref, qcat_ref, m_ref, r_ref, l_ref, acc_ref,
                 s0_ref, s1_ref, s2_ref, s3_ref, p0_ref, p1_ref, p2_ref, p3_ref, *, n_items):
    BLK = MOBA_BLOCK
    DH = MOBA_DH
    G = MOBA_STAGE
    STG = G * BLK
    nb = vt_ref.shape[2]
    nbp = kbar_ref.shape[0]
    s_bufs = (s0_ref, s1_ref, s2_ref, s3_ref)
    p_bufs = (p0_ref, p1_ref, p2_ref, p3_ref)
    lane = lax.broadcasted_iota(jnp.int32, (1, LANES), 1)

    def q_cat_body(i, c):
        q_pair = q_ref[pl.ds(pl.multiple_of(i * BLK, BLK), BLK), :]
        zero = jnp.zeros_like(q_pair)
        qcat_ref[i, :BLK, :] = jnp.where(lane < DH, q_pair, zero)
        qcat_ref[i, BLK:, :] = jnp.where(lane >= DH, q_pair, zero)
        return c

    lax.fori_loop(0, nb, q_cat_body, 0, unroll=4)

    def q_cat_of(i):
        return qcat_ref[i]

    kbar_ref[...] = jnp.zeros_like(kbar_ref)

    def kbar_body(n, c):
        kb = k_ref[pl.ds(pl.multiple_of(n * BLK, BLK), BLK), :]
        kbar_ref[pl.ds(n, 1), :] = jnp.mean(kb.astype(F32), axis=0, keepdims=True)
        return c

    lax.fori_loop(0, nb, kbar_body, 0, unroll=4)

    kbar = kbar_ref[...]
    kbar_hi = kbar.astype(BF16)
    kbar_lo = (kbar - kbar_hi.astype(F32)).astype(BF16)
    blk_idx = lax.broadcasted_iota(jnp.int32, (nbp, 2 * BLK), 0)

    def select_blocks(i):
        q_cat = q_cat_of(i)
        gate = _dot_nt(kbar_hi, q_cat) + _dot_nt(kbar_lo, q_cat)
        past = blk_idx < i
        gate = jnp.where(past, gate, NEG)
        sel = jnp.zeros((nbp, 2 * BLK), jnp.bool_)
        for _ in range(MOBA_TOPK):
            best = jnp.max(gate, axis=0, keepdims=True)
            first = jnp.min(jnp.where(gate == best, blk_idx, nbp), axis=0, keepdims=True)
            pick = blk_idx == first
            sel = sel | (pick & past)
            gate = jnp.where(pick, -3e38, gate)
        bias_ref[i] = jnp.where(sel, 0.0, NEG)

    def select_body(u, c):
        select_blocks(2 * u)
        select_blocks(2 * u + 1)
        return c

    lax.fori_loop(0, nb // 2, select_body, 0, unroll=4)

    m_ref[...] = jnp.full(m_ref.shape, NEG, F32)
    r_ref[...] = jnp.full(r_ref.shape, NEG, F32)
    l_ref[...] = jnp.zeros_like(l_ref)
    acc_ref[...] = jnp.zeros_like(acc_ref)

    def scores(i, st, dst_ref, n_live):
        rows = n_live * BLK
        ks = k_ref[pl.ds(pl.multiple_of(st * STG, STG), rows), :]
        dst_ref[:rows, :] = _dot_nt(ks, q_cat_of(i))

    def values(st, p_ref, n_live):
        rows = n_live * BLK
        vt = jnp.concatenate([vt_ref[0, 0, G * st + g] for g in range(n_live)], axis=1)
        pv = _dot(jnp.concatenate([vt, jnp.ones((BF16_ROWS, rows), BF16)], axis=0), p_ref[:rows, :])
        return pv[:DH, :BLK], pv[DH:2 * DH, BLK:], pv[2 * DH:2 * DH + 1, :]

    def fold(pv, m_pv, i):
        num0, num1, sums = pv
        scale = jnp.exp2(r_ref[i] - m_pv)
        r_ref[i] = m_pv
        l_ref[i] = scale * l_ref[i] + sums
        acc_ref[i, :DH, :] = scale[:, :BLK] * acc_ref[i, :DH, :] + num0
        acc_ref[i, DH:, :] = scale[:, BLK:] * acc_ref[i, DH:, :] + num1

    key_idx = lax.broadcasted_iota(jnp.int32, (BLK, 2 * BLK), 0)
    qry_idx = lax.broadcasted_iota(jnp.int32, (BLK, 2 * BLK), 1) & (BLK - 1)
    causal = key_idx <= qry_idx

    def make_probs(own):
        def probs(i, st, s_ref, p_ref, tok):
            n_live = G if own is None else own + 1
            tiles, shifts = [], []
            m_new = m_ref[i]
            for g in range(n_live):
                sg = s_ref[g * BLK:(g + 1) * BLK, :]
                if g == own:
                    sg = jnp.where(causal, sg, NEG)
                    bg = None
                    m_new = jnp.maximum(m_new, jnp.max(sg, axis=0, keepdims=True))
                else:
                    bg = bias_ref[i, pl.ds(G * st + g, 1), :]
                    m_new = jnp.maximum(m_new, jnp.max(sg, axis=0, keepdims=True) + bg)
                tiles.append(sg)
                shifts.append(bg)
            m_ref[i] = m_new
            m_new = m_new + tok
            for g in range(n_live):
                shift = m_new if shifts[g] is None else m_new - shifts[g]
                p_ref[g * BLK:(g + 1) * BLK, :] = jnp.exp2(tiles[g] - shift).astype(BF16)
            return m_new
        return probs

    def run_items(count, item, probs, tok0, n_live):
        def clamped(t):
            return item(jnp.clip(t, 0, count - 1))

        scores(*clamped(0), s_bufs[0], n_live)
        scores(*clamped(1), s_bufs[1], n_live)

        def slot(t, k, carry, look_ahead):
            m_prev, toks = carry
            if look_ahead:
                scores(*clamped(t + 2), s_bufs[(k + 2) % 4], n_live)
            i_prev, st_prev = clamped(t - 1)
            pv = values(st_prev, p_bufs[(k - 1) % 4], n_live)
            i, st = clamped(t)
            m_t = probs(i, st, s_bufs[k % 4], p_bufs[k % 4], toks[0])
            fold(pv, m_prev, i_prev)
            return m_t, (toks[1], toks[2], pv[2] * 0.0)

        unroll = 4
        trips = count // unroll

        def body(it, carry, last=False):
            for k in range(unroll):
                carry = slot(unroll * it + k, k, carry, look_ahead=not (last and k + 2 >= unroll))
            return carry

        carry = lax.fori_loop(0, trips - 1, body, (r_ref[clamped(0)[0]], (tok0, tok0, tok0)))
        m_last, _ = body(trips - 1, carry, last=True)
        i_last, st_last = clamped(count - 1)
        pv_last = values(st_last, p_bufs[(count - 1) % 4], n_live)
        fold(pv_last, m_last, i_last)
        tok_last = pv_last[2] * 0.0
        p_bufs[3][...] = jnp.broadcast_to(tok_last.astype(BF16), p_bufs[3].shape)
        return tok_last

    p_bufs[3][...] = jnp.zeros_like(p_bufs[3])
    tok = jnp.zeros((1, 2 * BLK), F32)
    for own in range(G):
        tok = run_items(nb // G, lambda u, own=own: (G * u + own, u), make_probs(own), tok, own + 1)
    run_items(n_items, lambda t: (tile_tab_ref[t], stage_tab_ref[t]), make_probs(None), tok, G)

    def finish(i, c):
        l = l_ref[i]
        acc = acc_ref[i]
        out_t = jnp.concatenate([acc[:DH] / l[:, :BLK], acc[DH:] / l[:, BLK:]], axis=0)
        o_ref[pl.ds(pl.multiple_of(i * BLK, BLK), BLK), :] = out_t.T.astype(BF16)
        return c

    lax.fori_loop(0, nb, finish, 0, unroll=8)


def _moba(mq, mk, mvt, *, batch, seq):
    BLK = MOBA_BLOCK
    nb = seq // BLK
    G = MOBA_STAGE
    assert nb % (4 * G) == 0
    npair = MOBA_W // LANES
    items = [(i, st) for st in range(nb // G) for i in range(G * st + G, nb)]
    assert len(items) % 4 == 0
    tile_tab = jnp.asarray([i for i, _ in items], jnp.int32)
    stage_tab = jnp.asarray([st for _, st in items], jnp.int32)
    qkspec = pl.BlockSpec((seq, LANES), lambda b, hp, *_: (b, hp))
    vtspec = pl.BlockSpec((1, 1, nb, LANES, BLK), lambda b, hp, *_: (b, hp, 0, 0, 0))
    stat = pltpu.VMEM((nb, 1, 2 * BLK), F32)
    sbuf = pltpu.VMEM((G * BLK, 2 * BLK), F32)
    pbuf = pltpu.VMEM((G * BLK, 2 * BLK), BF16)
    return pl.pallas_call(
        functools.partial(_moba_kernel, n_items=len(items)),
        grid_spec=pltpu.PrefetchScalarGridSpec(
            num_scalar_prefetch=2,
            grid=(batch, npair),
            in_specs=[qkspec, qkspec, vtspec],
            out_specs=qkspec,
            scratch_shapes=[pltpu.VMEM((nb, LANES), F32), pltpu.VMEM((nb, nb, 2 * BLK), F32),
                            pltpu.VMEM((nb, 2 * BLK, LANES), BF16),
                            stat, stat, stat, pltpu.VMEM((nb, LANES, BLK), F32),
                            sbuf, sbuf, sbuf, sbuf, pbuf, pbuf, pbuf, pbuf]),
        out_shape=jax.ShapeDtypeStruct((batch * seq, MOBA_W), BF16),
        compiler_params=pltpu.CompilerParams(dimension_semantics=("arbitrary",) * 2,
                                             vmem_limit_bytes=VMEM_LIMIT),
        name="moba",
    )(tile_tab, stage_tab, mq, mk, mvt)


def _merge_kernel(x_ref, ret_ref, moba_ref, ga_ref, gb_ref, wr_ref, wm_ref, wo_ref, o_ref):
    a = _dot(ret_ref[...], wr_ref[...])
    b = _dot(moba_ref[...], wm_ref[...])
    mix = ga_ref[...].astype(F32) * a + gb_ref[...].astype(F32) * b
    o_ref[...] = x_ref[...] + _dot(mix.astype(BF16), wo_ref[...])


def _merge(x2, ret, moba, sga, sgb, w_ret_out, w_moba_out, w_o, *, tm):
    T = x2.shape[0]
    row = lambda w: pl.BlockSpec((tm, w), lambda i: (i, 0))
    deep = lambda w: pl.BlockSpec((tm, w), lambda i: (i, 0), pipeline_mode=pl.Buffered(3))

    def outer(x_hbm, ret_hbm, moba_hbm, ga_hbm, gb_hbm, wr_ref, wm_ref, wo_ref, o_hbm):
        def body(x_ref, ret_ref, moba_ref, ga_ref, gb_ref, o_ref):
            _merge_kernel(x_ref, ret_ref, moba_ref, ga_ref, gb_ref, wr_ref, wm_ref, wo_ref, o_ref)

        pltpu.emit_pipeline(
            body, grid=(T // tm,),
            in_specs=[deep(D_MODEL), deep(RET_V), deep(MOBA_W), deep(D_MODEL), deep(D_MODEL)],
            out_specs=[row(D_MODEL)],
        )(x_hbm, ret_hbm, moba_hbm, ga_hbm, gb_hbm, o_hbm)

    hbm = pl.BlockSpec(memory_space=pl.ANY)
    vmem = pl.BlockSpec(memory_space=pltpu.VMEM)
    return pl.pallas_call(
        outer,
        in_specs=[hbm] * 5 + [vmem] * 3,
        out_specs=hbm,
        out_shape=jax.ShapeDtypeStruct((T, D_MODEL), F32),
        compiler_params=pltpu.CompilerParams(vmem_limit_bytes=VMEM_LIMIT),
        name="merge",
    )(x2, ret, moba, sga, sgb, w_ret_out.astype(BF16), w_moba_out.astype(BF16), w_o.astype(BF16))


def _ffn_kernel(x_ref, n2_ref, wg_ref, wu_ref, wd_ref, o_ref):
    x = x_ref[...]
    ms = jnp.mean(x * x, axis=-1, keepdims=True)
    h = (x * lax.rsqrt(ms + RMS_EPS) * n2_ref[...]).astype(BF16)
    out = x
    for c0, c1 in FFN_COL_CHUNKS:
        g = _dot(h, wg_ref[:, c0:c1])
        u = _dot(h, wu_ref[:, c0:c1])
        act = (g * jax.nn.sigmoid(g) * u).astype(BF16)
        out = out + _dot(act, wd_ref[c0:c1, :])
    o_ref[...] = out


def _ffn(x2, norm2_w, w_gate, w_up, w_down, *, tm):
    T = x2.shape[0]
    row = pl.BlockSpec((tm, D_MODEL), lambda i: (i, 0))
    return pl.pallas_call(
        _ffn_kernel,
        grid=(T // tm,),
        in_specs=[row, _const_spec((1, D_MODEL)), _const_spec((D_MODEL, FFN_HIDDEN)),
                  _const_spec((D_MODEL, FFN_HIDDEN)), _const_spec((FFN_HIDDEN, D_MODEL))],
        out_specs=row,
        out_shape=jax.ShapeDtypeStruct((T, D_MODEL), F32),
        compiler_params=pltpu.CompilerParams(dimension_semantics=("arbitrary",), vmem_limit_bytes=VMEM_LIMIT),
        name="ffn",
    )(x2, norm2_w.astype(F32)[None, :], w_gate.astype(BF16), w_up.astype(BF16), w_down.astype(BF16))


def kernel(x, norm1_w, w_in, q_norm_w, k_norm_w, w_ret_out, w_moba_out, w_o, norm2_w, w_ffn_gate, w_ffn_up, w_ffn_down):
    B, S, D = x.shape
    assert D == D_MODEL and S % MOBA_BLOCK == 0 and S % MERGE_ROW_TILE == 0 and S % ROW_TILE == 0
    depth = norm1_w.shape[0]
    x2 = x.reshape(B * S, D)
    for l in range(depth):
        rq, rk, rv, rgs, mq, mk, mvt, sga, sgb = _inproj(
            x2, norm1_w[l], w_in[l], q_norm_w[l], k_norm_w[l], seq=S, tm=ROW_TILE)
        ret = _retention(rq, rk, rv, rgs, batch=B, seq=S, rt=ROW_TILE)
        moba = _moba(mq, mk, mvt, batch=B, seq=S)
        x2 = _merge(x2, ret, moba, sga, sgb, w_ret_out[l], w_moba_out[l], w_o[l], tm=MERGE_ROW_TILE)
        x2 = _ffn(x2, norm2_w[l], w_ffn_gate[l], w_ffn_up[l], w_ffn_down[l], tm=FFN_ROW_TILE)
    return x2.reshape(B, S, D)
```
